```python
import math
import jax, jax.numpy as jnp
from jax import lax
import numpy as np

D_MODEL = 2048
BATCH = 8
SEQ = 4096
DEPTH = 4

GRID_W = 64
CTX_LEN = 256
NORM_EPS = 1e-6
N_MOD = 6
ATT_HEAD_DIM = 128
ATT_HEADS = D_MODEL // 2 // ATT_HEAD_DIM
ATT_KV_HEADS = ATT_HEADS // 4
ATT_WINDOW = 128
ATT_BLOCK = 128
ROPE_THETA = 10000.0
SSD_HEAD_DIM = 64
SSD_W = D_MODEL // 4
SSD_HEADS = SSD_W // SSD_HEAD_DIM
SSD_GROUPS = 2
SSD_STATE = 128
SSD_CONV = 3
SSD_CHUNK = 128
SC_W = D_MODEL // 4
SC_CONV = 3
ATT_W = ATT_HEADS * ATT_HEAD_DIM
ATT_KV_W = ATT_KV_HEADS * ATT_HEAD_DIM
SSD_BC_W = SSD_GROUPS * SSD_STATE
SSD_XBC_W = SSD_W + 2 * SSD_BC_W
SSD_DT_W = 2 * SSD_HEADS
MIX_W = ATT_W + SSD_W + SC_W
IN_SPLITS = (ATT_W, ATT_KV_W, ATT_KV_W, SSD_W, SSD_XBC_W, SSD_DT_W, SC_W, SC_W, SC_W)
IN_W = sum(IN_SPLITS)
N_EXPERTS = 16
EC_FACTOR = 2
EXPERT_FF = D_MODEL // 2

kernel_name = 'hybrid_ssd_swa_shortconv_ec_moe_dit'

f32 = jnp.float32


def _rmsnorm(x, g):
    xf = x.astype(f32)
    y = xf * lax.rsqrt(jnp.mean(xf * xf, axis=-1, keepdims=True) + NORM_EPS)
    return (y * g.astype(f32)).astype(x.dtype)


def _modulate(x, g, shift, scale):
    return _rmsnorm(x, g) * (1 + scale) + shift


def _dwconv(x, w):
    k = w.shape[0]
    return lax.conv_general_dilated(x, w[:, None, :].astype(x.dtype), window_strides=(1,),
                                    padding=[(k // 2, k // 2)],
                                    dimension_numbers=('NWC', 'WIO', 'NWC'),
                                    feature_group_count=x.shape[-1])


def _rope_tables(n_tokens):
    rows = n_tokens // GRID_W
    row = jnp.repeat(jnp.arange(rows), GRID_W).astype(f32)
    col = jnp.tile(jnp.arange(GRID_W), rows).astype(f32)
    n_freq = ATT_HEAD_DIM // 4
    inv = ROPE_THETA ** (-jnp.arange(n_freq, dtype=f32) / n_freq)
    ang_r = (row[:, None] * inv)[:, None, :]
    ang_c = (col[:, None] * inv)[:, None, :]
    return (jnp.cos(ang_r), jnp.sin(ang_r), jnp.cos(ang_c), jnp.sin(ang_c))


def _rot_half(x, cos, sin):
    x1, x2 = jnp.split(x, 2, axis=-1)
    return jnp.concatenate([x1 * cos - x2 * sin, x2 * cos + x1 * sin], axis=-1)


def _rope_2d(x, tabs):
    cos_r, sin_r, cos_c, sin_c = (t.astype(x.dtype) for t in tabs)
    xr, xc = jnp.split(x, 2, axis=-1)
    return jnp.concatenate([_rot_half(xr, cos_r, sin_r), _rot_half(xc, cos_c, sin_c)], axis=-1)


def _attn_latent(q, k, v, kc, vc, sink):
    b, s, _, dh = q.shape
    l = kc.shape[1]
    nb = s // ATT_BLOCK
    grp = ATT_HEADS // ATT_KV_HEADS
    scale = dh ** -0.5
    qb = q.reshape(b, nb, ATT_BLOCK, ATT_KV_HEADS, grp, dh)

    def band(t):
        tp = jnp.pad(t, ((0, 0), (ATT_BLOCK, ATT_BLOCK), (0, 0), (0, 0)))
        tp = tp.reshape(b, nb + 2, ATT_BLOCK, ATT_KV_HEADS, dh)
        return jnp.concatenate([tp[:, :-2], tp[:, 1:-1], tp[:, 2:]], axis=2)

    kb, vb = band(k), band(v)
    qpos = jnp.arange(s).reshape(nb, ATT_BLOCK)
    kpos = (jnp.arange(nb)[:, None] - 1) * ATT_BLOCK + jnp.arange(3 * ATT_BLOCK)[None, :]
    mask = ((jnp.abs(qpos[:, :, None] - kpos[:, None, :]) <= ATT_WINDOW)
            & (kpos[:, None, :] >= 0) & (kpos[:, None, :] < s))
    s_win = jnp.einsum('bnqhgd,bnkhd->bnhgqk', qb, kb).astype(f32) * scale
    s_win = jnp.where(mask[None, :, None, None], s_win, -jnp.inf)
    s_ctx = jnp.einsum('bnqhgd,bkhd->bnhgqk', qb, kc).astype(f32) * scale
    s_sink = jnp.broadcast_to(sink.astype(f32).reshape(ATT_KV_HEADS, grp, 1, 1), s_win.shape[:-1] + (1,))
    p = jax.nn.softmax(jnp.concatenate([s_win, s_ctx, s_sink], axis=-1), axis=-1).astype(q.dtype)
    w = 3 * ATT_BLOCK
    o = (jnp.einsum('bnhgqk,bnkhd->bnqhgd', p[..., :w], vb)
         + jnp.einsum('bnhgqk,bkhd->bnqhgd', p[..., w:w + l], vc))
    return o.reshape(b, s, ATT_HEADS * dh)


def _attn_context(qc, kc, vc, sink):
    b, l, _, dh = qc.shape
    grp = ATT_HEADS // ATT_KV_HEADS
    qg = qc.reshape(b, l, ATT_KV_HEADS, grp, dh)
    sc = jnp.einsum('bqhgd,bkhd->bhgqk', qg, kc).astype(f32) * (dh ** -0.5)
    s_sink = jnp.broadcast_to(sink.astype(f32).reshape(ATT_KV_HEADS, grp, 1, 1), sc.shape[:-1] + (1,))
    p = jax.nn.softmax(jnp.concatenate([sc, s_sink], axis=-1), axis=-1)[..., :l].astype(qc.dtype)
    o = jnp.einsum('bhgqk,bkhd->bqhgd', p, vc)
    return o.reshape(b, l, ATT_HEADS * dh)


def _ssd_scan(xs, dt, a, bm, cm, h0):
    b, t = xs.shape[:2]
    nc = t // SSD_CHUNK
    ch = lambda u: u.reshape((b, nc, SSD_CHUNK) + u.shape[2:])
    xs, dt, bm, cm = ch(xs), ch(dt), ch(bm), ch(cm)
    cum = jnp.cumsum(dt * a, axis=2)
    xdt = xs * dt[..., None]
    tril = jnp.tril(jnp.ones((SSD_CHUNK, SSD_CHUNK), bool))
    seg = cum[:, :, :, None] - cum[:, :, None, :]
    decay = jnp.exp(jnp.where(tril[:, :, None, None], seg, -jnp.inf))
    cb = jnp.einsum('bcqgn,bcsgn->bcqsg', cm, bm)
    y_diag = jnp.einsum('bcqsgk,bcsgkp->bcqgkp', cb[..., None] * decay, xdt)
    decay_end = jnp.exp(cum[:, :, -1:] - cum)
    st = jnp.einsum('bcsgn,bcsgkp->bcgkpn', bm, xdt * decay_end[..., None])
    chunk_decay = jnp.exp(cum[:, :, -1])

    def step(h, inp):
        s_c, d_c = inp
        return h * d_c[..., None, None] + s_c, h

    h_last, h_prev = lax.scan(step, h0, (jnp.moveaxis(st, 1, 0), jnp.moveaxis(chunk_decay, 1, 0)))
    h_prev = jnp.moveaxis(h_prev, 0, 1)
    y_off = jnp.einsum('bcqgn,bcgkpn->bcqgkp', cm, h_prev) * jnp.exp(cum)[..., None]
    return (y_diag + y_off).reshape((b, t) + xs.shape[3:]), h_last


def _ssd_bidir(xbc_raw, dt_raw, conv_w, conv_b, a_log, dt_bias, d_skip, h0):
    b, t, _ = xbc_raw.shape
    kh = SSD_HEADS // SSD_GROUPS
    xbc = jax.nn.silu(_dwconv(xbc_raw, conv_w) + conv_b).astype(f32)
    xs, bm, cm = jnp.split(xbc, [SSD_W, SSD_W + SSD_BC_W], axis=-1)
    xs = xs.reshape(b, t, SSD_GROUPS, kh, SSD_HEAD_DIM)
    bm = bm.reshape(b, t, SSD_GROUPS, SSD_STATE)
    cm = cm.reshape(b, t, SSD_GROUPS, SSD_STATE)
    dt = jax.nn.softplus(dt_raw.astype(f32).reshape(b, t, 2, SSD_GROUPS, kh)
                         + dt_bias.astype(f32).reshape(2, SSD_GROUPS, kh))
    a = -jnp.exp(a_log.astype(f32).reshape(2, SSD_GROUPS, kh))
    flip = lambda u: jnp.flip(u, axis=1)
    y_f, h_f = _ssd_scan(xs, dt[:, :, 0], a[0], bm, cm, h0[0])
    y_b, h_b = _ssd_scan(flip(xs), flip(dt[:, :, 1]), a[1], flip(bm), flip(cm), h0[1])
    y = y_f + flip(y_b) + d_skip.astype(f32).reshape(SSD_GROUPS, kh, 1) * xs
    return y.reshape(b, t, SSD_W), jnp.stack([h_f, h_b])


def _ssd_out(y, z, g):
    b, t, _ = y.shape
    u = (y * jax.nn.silu(z.astype(f32))).reshape(b, t, SSD_GROUPS, SSD_W // SSD_GROUPS)
    u = u * lax.rsqrt(jnp.mean(u * u, axis=-1, keepdims=True) + NORM_EPS)
    return (u.reshape(b, t, SSD_W) * g.astype(f32)).astype(z.dtype)


def _ec_moe(h, w_r, w_g, w_u, w_d):
    b, t, d = h.shape
    cap = EC_FACTOR * t // N_EXPERTS
    aff = jax.nn.softmax((h @ w_r).astype(f32), axis=-1)
    gval, idx = lax.top_k(jnp.swapaxes(aff, 1, 2), cap)
    xg = jax.vmap(lambda hb, ib: hb[ib])(h, idx)
    a = jnp.einsum('becd,edf->becf', xg, w_g)
    u = jnp.einsum('becd,edf->becf', xg, w_u)
    y = jnp.einsum('becf,efd->becd', jax.nn.silu(a) * u, w_d) * gval[..., None].astype(h.dtype)
    return jax.vmap(lambda yb, ib: jnp.zeros((t, d), h.dtype).at[ib.reshape(-1)].add(yb.reshape(-1, d)))(y, idx)


def _layer(x, xc, c, c_ctx, rope, with_ctx_out, norm1_g, norm2_g, w_mod, b_mod, w_in, q_norm_g, k_norm_g,
           attn_sink, ssd_conv_w, ssd_conv_b, ssd_dt_bias, ssd_a_log, ssd_d, ssd_norm_g, sc_conv_w, w_out,
           w_router, w_eg, w_eu, w_ed):
    b = x.shape[0]
    sh1, sc1, g1, sh2, sc2, g2 = [m[:, None, :] for m in jnp.split(jax.nn.silu(c) @ w_mod + b_mod, N_MOD, axis=-1)]
    sh1c, sc1c, g1c, sh2c, sc2c, g2c = jnp.split(jax.nn.silu(c_ctx) @ w_mod + b_mod, N_MOD, axis=-1)
    cuts = [int(i) for i in np.cumsum(IN_SPLITS)[:-1]]
    h = _modulate(x, norm1_g, sh1, sc1)
    hc = _modulate(xc, norm1_g, sh1c, sc1c)
    q, k, v, z, xbc, dtr, scb, scc, sch = jnp.split(h @ w_in, cuts, axis=-1)
    qc, kc, vc, zc, xbcc, dtrc, scbc, sccc, schc = jnp.split(hc @ w_in, cuts, axis=-1)
    heads = lambda u, n: u.reshape(u.shape[:2] + (n, ATT_HEAD_DIM))

    kc_h = _rmsnorm(heads(kc, ATT_KV_HEADS), k_norm_g)
    vc_h = heads(vc, ATT_KV_HEADS)
    q_h = _rope_2d(_rmsnorm(heads(q, ATT_HEADS), q_norm_g), rope)
    k_h = _rope_2d(_rmsnorm(heads(k, ATT_KV_HEADS), k_norm_g), rope)
    att = _attn_latent(q_h, k_h, heads(v, ATT_KV_HEADS), kc_h, vc_h, attn_sink)

    h0 = jnp.zeros((2, b, SSD_GROUPS, SSD_HEADS // SSD_GROUPS, SSD_HEAD_DIM, SSD_STATE), f32)
    yc_raw, h_ctx = _ssd_bidir(xbcc, dtrc, ssd_conv_w, ssd_conv_b, ssd_a_log, ssd_dt_bias, ssd_d, h0)
    y_raw, _ = _ssd_bidir(xbc, dtr, ssd_conv_w, ssd_conv_b, ssd_a_log, ssd_dt_bias, ssd_d, h_ctx)
    ssd = _ssd_out(y_raw, z, ssd_norm_g)

    sconv = scb * _dwconv(scc * sch, sc_conv_w)

    x = x + g1 * (jnp.concatenate([att, ssd, sconv], axis=-1) @ w_out)
    x = x + g2 * _ec_moe(_modulate(x, norm2_g, sh2, sc2), w_router, w_eg, w_eu, w_ed)

    if with_ctx_out:
        attc = _attn_context(_rmsnorm(heads(qc, ATT_HEADS), q_norm_g), kc_h, vc_h, attn_sink)
        ssdc = _ssd_out(yc_raw, zc, ssd_norm_g)
        sconvc = scbc * _dwconv(sccc * schc, sc_conv_w)
        xc = xc + g1c * (jnp.concatenate([attc, ssdc, sconvc], axis=-1) @ w_out)
        xc = xc + g2c * _ec_moe(_modulate(xc, norm2_g, sh2c, sc2c), w_router, w_eg, w_eu, w_ed)
    return x, xc


def setup_inputs(seed: int = 0) -> dict:
    key = jax.random.key(seed)
    ks = iter(jax.random.split(key, 32))
    nrm = lambda shape, scale: jax.random.normal(next(ks), shape, f32) * scale
    L, D = DEPTH, D_MODEL
    dt0 = jnp.exp(jax.random.uniform(next(ks), (L, 2, SSD_HEADS), f32, math.log(1e-3), math.log(1e-1)))
    dt_bias = dt0 + jnp.log(-jnp.expm1(-dt0))
    a_log = jnp.log(jax.random.uniform(next(ks), (L, 2, SSD_HEADS), f32, 1.0, 16.0))
    return {
        'x': nrm((BATCH, SEQ, D), 1.0),
        'c': nrm((BATCH, D), 1.0),
        'ctx': nrm((BATCH, CTX_LEN, D), 1.0),
        'c_ctx': nrm((D,), 1.0),
        'norm1_g': 1.0 + nrm((L, D), 0.02),
        'norm2_g': 1.0 + nrm((L, D), 0.02),
        'w_mod': nrm((L, D, N_MOD * D), 0.5 * D ** -0.5),
        'b_mod': nrm((L, N_MOD * D), 0.02),
        'w_in': nrm((L, D, IN_W), D ** -0.5),
        'q_norm_g': 1.0 + nrm((L, ATT_HEAD_DIM), 0.02),
        'k_norm_g': 1.0 + nrm((L, ATT_HEAD_DIM), 0.02),
        'attn_sink': nrm((L, ATT_HEADS), 0.5),
        'ssd_conv_w': nrm((L, SSD_CONV, SSD_XBC_W), SSD_CONV ** -0.5),
        'ssd_conv_b': nrm((L, SSD_XBC_W), 0.02),
        'ssd_dt_bias': dt_bias,
        'ssd_a_log': a_log,
        'ssd_d': 1.0 + nrm((L, SSD_HEADS), 0.1),
        'ssd_norm_g': 1.0 + nrm((L, SSD_W), 0.02),
        'sc_conv_w': nrm((L, SC_CONV, SC_W), SC_CONV ** -0.5),
        'w_out': nrm((L, MIX_W, D), MIX_W ** -0.5),
        'w_router': nrm((L, D, N_EXPERTS), D ** -0.5),
        'w_expert_gate': nrm((L, N_EXPERTS, D, EXPERT_FF), D ** -0.5),
        'w_expert_up': nrm((L, N_EXPERTS, D, EXPERT_FF), D ** -0.5),
        'w_expert_down': nrm((L, N_EXPERTS, EXPERT_FF, D), EXPERT_FF ** -0.5),
    }


def reference(x, c, ctx, c_ctx, norm1_g, norm2_g, w_mod, b_mod, w_in, q_norm_g, k_norm_g, attn_sink,
              ssd_conv_w, ssd_conv_b, ssd_dt_bias, ssd_a_log, ssd_d, ssd_norm_g, sc_conv_w, w_out,
              w_router, w_expert_gate, w_expert_up, w_expert_down):
    rope = _rope_tables(x.shape[1])
    xc = ctx
    for i in range(DEPTH):
        x, xc = _layer(x, xc, c, c_ctx, rope, i < DEPTH - 1, norm1_g[i], norm2_g[i], w_mod[i], b_mod[i],
                       w_in[i], q_norm_g[i], k_norm_g[i], attn_sink[i], ssd_conv_w[i], ssd_conv_b[i],
                       ssd_dt_bias[i], ssd_a_log[i], ssd_d[i], ssd_norm_g[i], sc_conv_w[i], w_out[i],
                       w_router[i], w_expert_gate[i], w_expert_up[i], w_expert_down[i])
    return x
```

```python
import functools

import jax
import jax.numpy as jnp
from jax import lax
from jax.experimental import pallas as pl
from jax.experimental.pallas import tpu as pltpu

f32 = jnp.float32
bf16 = jnp.bfloat16

D_MODEL = 2048
DEPTH = 4
GRID_W = 64
NORM_EPS = 1e-6
N_MOD = 6
HEAD_DIM = 128
ATT_HEADS = 8
ATT_KV_HEADS = 2
ATT_GROUP = ATT_HEADS // ATT_KV_HEADS
ATT_WINDOW = 128
ATT_BLOCK = 128
ROPE_THETA = 10000.0
SSD_HEAD_DIM = 64
SSD_W = 512
SSD_HEADS = 8
SSD_GROUPS = 2
SSD_KH = SSD_HEADS // SSD_GROUPS
SSD_STATE = 128
SSD_CHUNK = 128
SSD_GW = SSD_W // SSD_GROUPS
SC_W = 512
ATT_W = ATT_HEADS * HEAD_DIM
ATT_KV_W = ATT_KV_HEADS * HEAD_DIM
SSD_BC_W = SSD_GROUPS * SSD_STATE
SSD_XBC_W = SSD_W + 2 * SSD_BC_W
SSD_DT_W = 2 * SSD_HEADS
N_EXPERTS = 16
EC_FACTOR = 2
EXPERT_FF = 1024

LANES = 128
DT_PAD = SSD_GROUPS * LANES
ROUTER_PAD = LANES
SEG_W = (ATT_W, ATT_KV_W, ATT_KV_W, SSD_W, SSD_XBC_W, DT_PAD, SC_W, SC_W, SC_W)
SEG_DT = (bf16, bf16, bf16, bf16, bf16, f32, bf16, bf16, bf16)
IN_W_PAD = sum(SEG_W)
VMEM_LIMIT = 56 * 1024 * 1024
NEG_INF = float("-inf")
FFN_MAX_TILE = 576


def _cparams(sem):
    return pltpu.CompilerParams(dimension_semantics=sem, vmem_limit_bytes=VMEM_LIMIT)


def _silu(v):
    return v * jax.nn.sigmoid(v)


def _shifted_rows(pad_ref, r0, n):
    x0 = pad_ref[pl.ds(r0 + 8, n), :]
    before = pad_ref[pl.ds(r0, 8), :][7:8, :]
    after = pad_ref[pl.ds(r0 + 8 + n, 8), :][0:1, :]
    ri = lax.broadcasted_iota(jnp.int32, x0.shape, 0)
    xm = jnp.where(ri == 0, before, pltpu.roll(x0, 1, 0))
    xp = jnp.where(ri == n - 1, after, pltpu.roll(x0, n - 1, 0))
    return xm, x0, xp


def _split3(v):
    hi = v.astype(bf16)
    r = v - hi.astype(f32)
    mid = r.astype(bf16)
    lo = (r - mid.astype(f32)).astype(bf16)
    return hi, mid, lo


def _mod_kernel(c_ref, w_ref, b_ref, o_ref):
    a = _silu(c_ref[...]).astype(bf16)
    o_ref[0] = jnp.dot(a, w_ref[0].astype(bf16), preferred_element_type=f32) + b_ref[0]


def _modulations(cvec, w_mod, b_mod):
    nl, d, n = w_mod.shape
    r = cvec.shape[0]
    tn = 1024
    return pl.pallas_call(
        _mod_kernel,
        grid=(nl, n // tn),
        in_specs=[pl.BlockSpec((r, d), lambda l, j: (0, 0)),
                  pl.BlockSpec((1, d, tn), lambda l, j: (l, 0, j)),
                  pl.BlockSpec((1, 1, tn), lambda l, j: (l, 0, j))],
        out_specs=pl.BlockSpec((1, r, tn), lambda l, j: (l, 0, j)),
        out_shape=jax.ShapeDtypeStruct((nl, r, n), f32),
        compiler_params=_cparams(("arbitrary", "arbitrary")),
        name="modulations",
    )(cvec, w_mod, b_mod.reshape(nl, 1, n))


def _inproj_kernel(x_ref, sh_ref, sc_ref, g_ref, w_ref, *rest):
    outs, h_scr = rest[:-1], rest[-1]
    x = x_ref[...]
    y = x * lax.rsqrt(jnp.mean(x * x, axis=-1, keepdims=True) + NORM_EPS) * g_ref[...]
    h_scr[...] = (y * (1.0 + sc_ref[0]) + sh_ref[0]).astype(bf16)
    off = 0
    for ref, width in zip(outs, SEG_W):
        for c0 in range(0, width, 512):
            cw = min(512, width - c0)
            ref[:, c0:c0 + cw] = jnp.dot(h_scr[...], w_ref[:, off + c0:off + c0 + cw],
                                         preferred_element_type=f32).astype(ref.dtype)
        off += width


def _inproj(x2d, shift, scale, g, w, tm, rows_per_mod):
    m, d = x2d.shape
    tpm = rows_per_mod // tm
    mod_spec = pl.BlockSpec((1, 1, d), lambda i: (i // tpm, 0, 0))
    return pl.pallas_call(
        _inproj_kernel,
        grid=(m // tm,),
        in_specs=[pl.BlockSpec((tm, d), lambda i: (i, 0)), mod_spec, mod_spec,
                  pl.BlockSpec((1, d), lambda i: (0, 0)),
                  pl.BlockSpec((d, IN_W_PAD), lambda i: (0, 0), pipeline_mode=pl.Buffered(1))],
        out_specs=[pl.BlockSpec((tm, wd), lambda i: (i, 0)) for wd in SEG_W],
        out_shape=[jax.ShapeDtypeStruct((m, wd), dt) for wd, dt in zip(SEG_W, SEG_DT)],
        scratch_shapes=[pltpu.VMEM((tm, d), bf16)],
        compiler_params=_cparams(("arbitrary",)),
        name="inproj",
    )(x2d, shift, scale, g, w)


def _norm_rope(v, g, cos, sins):
    y = v * lax.rsqrt(jnp.mean(v * v, axis=-1, keepdims=True) + NORM_EPS) * g
    if cos is None:
        return y
    lane = lax.broadcasted_iota(jnp.int32, y.shape, 1)
    quarter = HEAD_DIM // 4
    partner = jnp.where((lane % (2 * quarter)) < quarter,
                        pltpu.roll(y, HEAD_DIM - quarter, 1), pltpu.roll(y, quarter, 1))
    return y * cos + partner * sins


def _kprep_kernel(k_ref, g_ref, cos_ref, sin_ref, o_ref, *, rope):
    for hh in range(ATT_KV_HEADS):
        sl = slice(hh * HEAD_DIM, (hh + 1) * HEAD_DIM)
        v = k_ref[:, sl].astype(f32)
        o_ref[:, sl] = _norm_rope(v, g_ref[...], cos_ref[...] if rope else None,
                                  sin_ref[...] if rope else None).astype(bf16)


def _kprep(k2d, g, cos, sins, seq, rope):
    m = k2d.shape[0]
    tk = min(512, seq)
    nt = seq // tk
    tab = pl.BlockSpec((tk, HEAD_DIM), lambda i: (i % nt, 0))
    return pl.pallas_call(
        functools.partial(_kprep_kernel, rope=rope),
        grid=(m // tk,),
        in_specs=[pl.BlockSpec((tk, ATT_KV_W), lambda i: (i, 0)),
                  pl.BlockSpec((1, HEAD_DIM), lambda i: (0, 0)), tab, tab],
        out_specs=pl.BlockSpec((tk, ATT_KV_W), lambda i: (i, 0)),
        out_shape=jax.ShapeDtypeStruct((m, ATT_KV_W), bf16),
        compiler_params=_cparams(("arbitrary",)),
        name="kprep_rope" if rope else "kprep",
    )(k2d, g, cos, sins)


def _attn_kernel(*refs, band, tq, seq):
    if band:
        (sink_ref, q_ref, cos_ref, sin_ref, qg_ref, kp_ref, km_ref, kn_ref, vp_ref, vm_ref, vn_ref,
         kc_ref, vc_ref, o_ref) = refs
    else:
        sink_ref, q_ref, qg_ref, kc_ref, vc_ref, o_ref = refs
    h = pl.program_id(1)
    n = pl.program_id(2)
    scale = HEAD_DIM ** -0.5
    kc = kc_ref[0]
    vc = vc_ref[0]
    if band:
        kwin = jnp.concatenate([kp_ref[0], km_ref[0], kn_ref[0]], axis=0)
        vwin = jnp.concatenate([vp_ref[0], vm_ref[0], vn_ref[0]], axis=0)
    rows = ATT_GROUP * ATT_BLOCK
    sinkcol = jnp.concatenate(
        [jnp.full((ATT_BLOCK, 1), sink_ref[h * ATT_GROUP + hh], f32) for hh in range(ATT_GROUP)], axis=0)
    nt = (((1,), (1,)), ((), ()))
    for jb in range(tq // ATT_BLOCK):
        r0 = jb * ATT_BLOCK
        qparts = []
        for hh in range(ATT_GROUP):
            qv = q_ref[0, r0:r0 + ATT_BLOCK, hh * HEAD_DIM:(hh + 1) * HEAD_DIM].astype(f32)
            if band:
                qv = _norm_rope(qv, qg_ref[...], cos_ref[r0:r0 + ATT_BLOCK, :], sin_ref[r0:r0 + ATT_BLOCK, :])
            else:
                qv = _norm_rope(qv, qg_ref[...], None, None)
            qparts.append(qv.astype(bf16))
        qs = jnp.concatenate(qparts, axis=0)
        s_c = lax.dot_general(qs, kc, nt, preferred_element_type=f32) * scale
        m = jnp.maximum(jnp.max(s_c, axis=-1, keepdims=True), sinkcol)
        if band:
            kb = kwin[r0:r0 + 3 * ATT_BLOCK]
            vb = vwin[r0:r0 + 3 * ATT_BLOCK]
            s_w = lax.dot_general(qs, kb, nt, preferred_element_type=f32) * scale
            ri = lax.broadcasted_iota(jnp.int32, (rows, 3 * ATT_BLOCK), 0) % ATT_BLOCK
            ci = lax.broadcasted_iota(jnp.int32, (rows, 3 * ATT_BLOCK), 1)
            kbase = n * tq + r0 - ATT_BLOCK
            lo = jnp.maximum(ri, -kbase)
            hi = jnp.minimum(ri + 2 * ATT_WINDOW, seq - 1 - kbase)
            s_w = jnp.where(ci >= lo, jnp.where(ci <= hi, s_w, NEG_INF), NEG_INF)
            m = jnp.maximum(m, jnp.max(s_w, axis=-1, keepdims=True))
            p_w = jnp.exp(s_w - m)
        p_c = jnp.exp(s_c - m)
        den = jnp.sum(p_c, axis=-1, keepdims=True) + jnp.exp(sinkcol - m)
        o = jnp.dot(p_c.astype(bf16), vc, preferred_element_type=f32)
        if band:
            den = den + jnp.sum(p_w, axis=-1, keepdims=True)
            o = o + jnp.dot(p_w.astype(bf16), vb, preferred_element_type=f32)
        o = o / den
        for hh in range(ATT_GROUP):
            o_ref[0, r0:r0 + ATT_BLOCK, hh * HEAD_DIM:(hh + 1) * HEAD_DIM] = (
                o[hh * ATT_BLOCK:(hh + 1) * ATT_BLOCK].astype(bf16))


def _attention(q, qg, sink, kc, vc, k=None, v=None, cos=None, sins=None):
    b, t, _ = q.shape
    lc = kc.shape[1]
    band = k is not None
    tq = 256
    gw = ATT_GROUP * HEAD_DIM
    nblk = t // ATT_BLOCK
    per = tq // ATT_BLOCK
    smem = pl.BlockSpec(memory_space=pltpu.SMEM)
    qspec = pl.BlockSpec((1, tq, gw), lambda bi, h, n: (bi, n, h))
    gspec = pl.BlockSpec((1, HEAD_DIM), lambda bi, h, n: (0, 0))
    cspec = pl.BlockSpec((1, lc, HEAD_DIM), lambda bi, h, n: (bi, 0, h))
    if band:
        tab = pl.BlockSpec((tq, HEAD_DIM), lambda bi, h, n: (n, 0))
        prev = pl.BlockSpec((1, ATT_BLOCK, HEAD_DIM), lambda bi, h, n: (bi, jnp.maximum(n * per - 1, 0), h))
        main = pl.BlockSpec((1, tq, HEAD_DIM), lambda bi, h, n: (bi, n, h))
        nxt = pl.BlockSpec((1, ATT_BLOCK, HEAD_DIM), lambda bi, h, n: (bi, jnp.minimum((n + 1) * per, nblk - 1), h))
        in_specs = [smem, qspec, tab, tab, gspec, prev, main, nxt, prev, main, nxt, cspec, cspec]
        args = (sink, q, cos, sins, qg, k, k, k, v, v, v, kc, vc)
    else:
        in_specs = [smem, qspec, gspec, cspec, cspec]
        args = (sink, q, qg, kc, vc)
    return pl.pallas_call(
        functools.partial(_attn_kernel, band=band, tq=tq, seq=t),
        grid=(b, ATT_KV_HEADS, t // tq),
        in_specs=in_specs,
        out_specs=pl.BlockSpec((1, tq, gw), lambda bi, h, n: (bi, n, h)),
        out_shape=jax.ShapeDtypeStruct((b, t, ATT_W), bf16),
        compiler_params=_cparams(("arbitrary", "arbitrary", "arbitrary")),
        name="attn_band" if band else "attn_ctx",
    )(*args)


def _ssd_kernel(xc_ref, bc_ref, cc_ref, dtc_ref, zc_ref, xl_ref, bl_ref, cl_ref, dtl_ref, zl_ref,
                cw_ref, cb_ref, dtb_ref, a_ref, dsk_ref, ng_ref, oc_ref, ol_ref,
                pad_scr, act_scr, dts_scr, y_scr, h_scr, *, lc, seq):
    ck = SSD_CHUNK
    gw = SSD_GW
    nst = SSD_STATE
    row = lax.broadcasted_iota(jnp.int32, (ck, ck), 0)
    col = lax.broadcasted_iota(jnp.int32, (ck, ck), 1)
    tri = (row >= col, col >= row)
    tri_bf = tuple(jnp.where(t, 1.0, 0.0).astype(bf16) for t in tri)
    h_scr[...] = jnp.zeros(h_scr.shape, f32)

    def run_seq(t, x_ref, b_ref, c_ref, dt_ref, z_ref, o_ref):
        nc = t // ck
        zero8 = jnp.zeros((8, gw + 2 * nst), f32)
        pad_scr[0:8, :] = zero8
        pad_scr[8 + t:16 + t, :] = zero8

        def fill(c, carry):
            r0 = pl.multiple_of(c * ck, ck)
            pad_scr[pl.ds(r0 + 8, ck), 0:gw] = x_ref[0, pl.ds(r0, ck), :].astype(f32)
            pad_scr[pl.ds(r0 + 8, ck), gw:gw + nst] = b_ref[0, pl.ds(r0, ck), :].astype(f32)
            pad_scr[pl.ds(r0 + 8, ck), gw + nst:gw + 2 * nst] = c_ref[0, pl.ds(r0, ck), :].astype(f32)
            return carry

        lax.fori_loop(0, nc, fill, 0)

        def conv(c, carry):
            r0 = pl.multiple_of(c * ck, ck)
            xm, x0, xp = _shifted_rows(pad_scr, r0, ck)
            act = _silu(cw_ref[0, 0:1, :] * xm + cw_ref[0, 1:2, :] * x0 + cw_ref[0, 2:3, :] * xp + cb_ref[0])
            act_scr[pl.ds(r0, ck), :] = act
            y_scr[pl.ds(r0, ck), :] = dsk_ref[...] * act[:, 0:gw]
            dv = dt_ref[0, pl.ds(r0, ck), :] + dtb_ref[0]
            dts_scr[pl.ds(r0, ck), :] = jnp.maximum(dv, 0.0) + jnp.log1p(jnp.exp(-jnp.abs(dv)))
            return carry

        lax.fori_loop(0, nc, conv, 0)

        def chunk(i, carry):
            for d in range(2):
                c = i if d == 0 else nc - 1 - i
                r0 = pl.multiple_of(c * ck, ck)
                xs = act_scr[pl.ds(r0, ck), 0:gw]
                bm = act_scr[pl.ds(r0, ck), gw:gw + nst]
                cm = act_scr[pl.ds(r0, ck), gw + nst:gw + 2 * nst]
                dt = dts_scr[pl.ds(r0, ck), :]
                hi, mid, lo = _split3(dt * a_ref[0])
                cum = (jnp.dot(tri_bf[d], hi, preferred_element_type=f32)
                       + jnp.dot(tri_bf[d], mid, preferred_element_type=f32)
                       + jnp.dot(tri_bf[d], lo, preferred_element_type=f32))
                tot = cum[ck - 1:ck, :] if d == 0 else cum[0:1, :]
                cum_t = cum.T
                dt_t = dt.T
                w_t = (dt * jnp.exp(tot - cum)).T
                ecum = jnp.exp(cum)
                etot = jnp.exp(tot)
                cb16 = cm.astype(bf16)
                cbm = lax.dot_general(cb16, bm.astype(bf16), (((1,), (1,)), ((), ())), preferred_element_type=f32)
                bm_t = bm.T
                for k in range(SSD_KH):
                    j = d * SSD_KH + k
                    hs = slice(k * SSD_HEAD_DIM, (k + 1) * SSD_HEAD_DIM)
                    seg = cum[:, j:j + 1] - cum_t[j:j + 1, :]
                    decay = jnp.exp(jnp.where(tri[d], seg, NEG_INF))
                    mk = (cbm * decay * dt_t[j:j + 1, :]).astype(bf16)
                    xk = xs[:, hs].astype(bf16)
                    hk = h_scr[j]
                    yk = (jnp.dot(mk, xk, preferred_element_type=f32)
                          + ecum[:, j:j + 1] * jnp.dot(cb16, hk.astype(bf16), preferred_element_type=f32))
                    y_scr[pl.ds(r0, ck), hs] = y_scr[pl.ds(r0, ck), hs] + yk
                    h_scr[j] = hk * etot[:, j:j + 1] + jnp.dot((bm_t * w_t[j:j + 1, :]).astype(bf16), xk,
                                                                preferred_element_type=f32)
            return carry

        lax.fori_loop(0, nc, chunk, 0)

        def gate(c, carry):
            r0 = pl.multiple_of(c * ck, ck)
            u = y_scr[pl.ds(r0, ck), :] * _silu(z_ref[0, pl.ds(r0, ck), :].astype(f32))
            u = u * lax.rsqrt(jnp.mean(u * u, axis=-1, keepdims=True) + NORM_EPS)
            o_ref[0, pl.ds(r0, ck), :] = (u * ng_ref[...]).astype(bf16)
            return carry

        lax.fori_loop(0, nc, gate, 0)

    run_seq(lc, xc_ref, bc_ref, cc_ref, dtc_ref, zc_ref, oc_ref)
    run_seq(seq, xl_ref, bl_ref, cl_ref, dtl_ref, zl_ref, ol_ref)


def _ssd(xbc_c, dt_c, z_c, xbc_l, dt_l, z_l, cw, cb, dtb, a_neg, dsk, ng):
    b, lc, _ = xbc_c.shape
    seq = xbc_l.shape[1]
    gw, nst = SSD_GW, SSD_STATE
    xoff = SSD_W // nst

    def seq_specs(t):
        return [pl.BlockSpec((1, t, gw), lambda bi, g: (bi, 0, g)),
                pl.BlockSpec((1, t, nst), lambda bi, g: (bi, 0, xoff + g)),
                pl.BlockSpec((1, t, nst), lambda bi, g: (bi, 0, xoff + SSD_GROUPS + g)),
                pl.BlockSpec((1, t, LANES), lambda bi, g: (bi, 0, g)),
                pl.BlockSpec((1, t, gw), lambda bi, g: (bi, 0, g))]

    cwid = gw + 2 * nst
    par_specs = [pl.BlockSpec((1, 3, cwid), lambda bi, g: (g, 0, 0)),
                 pl.BlockSpec((1, 1, cwid), lambda bi, g: (g, 0, 0)),
                 pl.BlockSpec((1, 1, LANES), lambda bi, g: (g, 0, 0)),
                 pl.BlockSpec((1, 1, LANES), lambda bi, g: (g, 0, 0)),
                 pl.BlockSpec((1, gw), lambda bi, g: (0, g)),
                 pl.BlockSpec((1, gw), lambda bi, g: (0, g))]
    return pl.pallas_call(
        functools.partial(_ssd_kernel, lc=lc, seq=seq),
        grid=(b, SSD_GROUPS),
        in_specs=seq_specs(lc) + seq_specs(seq) + par_specs,
        out_specs=[pl.BlockSpec((1, lc, gw), lambda bi, g: (bi, 0, g)),
                   pl.BlockSpec((1, seq, gw), lambda bi, g: (bi, 0, g))],
        out_shape=[jax.ShapeDtypeStruct((b, lc, SSD_W), bf16), jax.ShapeDtypeStruct((b, seq, SSD_W), bf16)],
        scratch_shapes=[pltpu.VMEM((seq + 16, cwid), f32), pltpu.VMEM((seq, cwid), f32),
                        pltpu.VMEM((seq, LANES), f32), pltpu.VMEM((seq, gw), f32),
                        pltpu.VMEM((2 * SSD_KH, nst, SSD_HEAD_DIM), f32)],
        compiler_params=_cparams(("arbitrary", "arbitrary")),
        name="ssd",
    )(xbc_c, xbc_c, xbc_c, dt_c, z_c, xbc_l, xbc_l, xbc_l, dt_l, z_l, cw, cb, dtb, a_neg, dsk, ng)


def _sconv_kernel(b_ref, c_ref, h_ref, w_ref, o_ref, pad_scr, *, t):
    ck = min(256, t)
    zero8 = jnp.zeros((8, LANES), f32)
    pad_scr[0:8, :] = zero8
    pad_scr[8 + t:16 + t, :] = zero8

    def fill(c, carry):
        r0 = pl.multiple_of(c * ck, ck)
        pad_scr[pl.ds(r0 + 8, ck), :] = c_ref[0, pl.ds(r0, ck), :].astype(f32) * h_ref[0, pl.ds(r0, ck), :].astype(f32)
        return carry

    lax.fori_loop(0, t // ck, fill, 0)

    def conv(c, carry):
        r0 = pl.multiple_of(c * ck, ck)
        xm, x0, xp = _shifted_rows(pad_scr, r0, ck)
        acc = w_ref[0:1, :] * xm + w_ref[1:2, :] * x0 + w_ref[2:3, :] * xp
        o_ref[0, pl.ds(r0, ck), :] = (b_ref[0, pl.ds(r0, ck), :].astype(f32) * acc).astype(bf16)
        return carry

    lax.fori_loop(0, t // ck, conv, 0)


def _sconv(scb, scc, sch, w):
    b, t, cw = scb.shape
    spec = pl.BlockSpec((1, t, LANES), lambda bi, j: (bi, 0, j))
    return pl.pallas_call(
        functools.partial(_sconv_kernel, t=t),
        grid=(b, cw // LANES),
        in_specs=[spec, spec, spec, pl.BlockSpec((3, LANES), lambda bi, j: (0, j))],
        out_specs=spec,
        out_shape=jax.ShapeDtypeStruct((b, t, cw), bf16),
        scratch_shapes=[pltpu.VMEM((t + 16, LANES), f32)],
        compiler_params=_cparams(("arbitrary", "arbitrary")),
        name="sconv",
    )(scb, scc, sch, w)


def _outproj_kernel(att_ref, ssd_ref, sc_ref, x_ref, g1_ref, sh2_ref, sc2_ref, n2_ref, wo_ref, wr_ref,
                    xo_ref, h2_ref, lg_ref):
    acc = jnp.dot(att_ref[...], wo_ref[0:ATT_W, :], preferred_element_type=f32)
    acc = acc + jnp.dot(ssd_ref[...], wo_ref[ATT_W:ATT_W + SSD_W, :], preferred_element_type=f32)
    acc = acc + jnp.dot(sc_ref[...], wo_ref[ATT_W + SSD_W:, :], preferred_element_type=f32)
    x = x_ref[...] + g1_ref[0] * acc
    xo_ref[...] = x
    y = x * lax.rsqrt(jnp.mean(x * x, axis=-1, keepdims=True) + NORM_EPS) * n2_ref[...]
    h2 = y * (1.0 + sc2_ref[0]) + sh2_ref[0]
    h2_ref[...] = h2.astype(bf16)
    hh, hm, hl = _split3(h2)
    w_hi, w_mid, w_lo = wr_ref[0], wr_ref[1], wr_ref[2]
    lg = jnp.dot(hh, w_hi, preferred_element_type=f32)
    lg = lg + jnp.dot(hm, w_hi, preferred_element_type=f32) + jnp.dot(hh, w_mid, preferred_element_type=f32)
    lg = lg + (jnp.dot(hl, w_hi, preferred_element_type=f32) + jnp.dot(hm, w_mid, preferred_element_type=f32)
               + jnp.dot(hh, w_lo, preferred_element_type=f32))
    lg_ref[...] = lg


def _outproj(att, ssd, sconv, x2d, g1, sh2, sc2, n2, wo, wr3, tm, rows_per_mod):
    m, d = x2d.shape
    tpm = rows_per_mod // tm
    mod_spec = pl.BlockSpec((1, 1, d), lambda i: (i // tpm, 0, 0))
    return pl.pallas_call(
        _outproj_kernel,
        grid=(m // tm,),
        in_specs=[pl.BlockSpec((tm, ATT_W), lambda i: (i, 0)), pl.BlockSpec((tm, SSD_W), lambda i: (i, 0)),
                  pl.BlockSpec((tm, SC_W), lambda i: (i, 0)), pl.BlockSpec((tm, d), lambda i: (i, 0)),
                  mod_spec, mod_spec, mod_spec, pl.BlockSpec((1, d), lambda i: (0, 0)),
                  pl.BlockSpec((d, d), lambda i: (0, 0)),
                  pl.BlockSpec((3, d, ROUTER_PAD), lambda i: (0, 0, 0))],
        out_specs=[pl.BlockSpec((tm, d), lambda i: (i, 0)), pl.BlockSpec((tm, d), lambda i: (i, 0)),
                   pl.BlockSpec((tm, ROUTER_PAD), lambda i: (i, 0))],
        out_shape=[jax.ShapeDtypeStruct((m, d), f32), jax.ShapeDtypeStruct((m, d), bf16),
                   jax.ShapeDtypeStruct((m, ROUTER_PAD), f32)],
        compiler_params=_cparams(("arbitrary",)),
        name="outproj",
    )(att, ssd, sconv, x2d, g1, sh2, sc2, n2, wo, wr3)


def _ffn_kernel(x_ref, gv_ref, wg_ref, wu_ref, wd_ref, o_ref):
    x = x_ref[0]
    a = jnp.dot(x, wg_ref[0], preferred_element_type=f32)
    u = jnp.dot(x, wu_ref[0], preferred_element_type=f32)
    hmid = (_silu(a) * u).astype(bf16)
    o_ref[0] = jnp.dot(hmid, wd_ref[0], preferred_element_type=f32) * gv_ref[0]


def _ffn_tile(nrows):
    for nt in range(1, nrows + 1):
        if nrows % nt == 0 and (nrows // nt) % 16 == 0 and nrows // nt <= FFN_MAX_TILE:
            return nrows // nt
    raise ValueError(f"no aligned row tile for {nrows} gathered rows")


def _expert_ffn(xg, gv, wg, wu, wd, tm):
    e, r, d = xg.shape
    ff = wg.shape[2]
    return pl.pallas_call(
        _ffn_kernel,
        grid=(e, r // tm),
        in_specs=[pl.BlockSpec((1, tm, d), lambda ei, i: (ei, i, 0)),
                  pl.BlockSpec((1, tm, 1), lambda ei, i: (ei, i, 0)),
                  pl.BlockSpec((1, d, ff), lambda ei, i: (ei, 0, 0)),
                  pl.BlockSpec((1, d, ff), lambda ei, i: (ei, 0, 0)),
                  pl.BlockSpec((1, ff, d), lambda ei, i: (ei, 0, 0))],
        out_specs=pl.BlockSpec((1, tm, d), lambda ei, i: (ei, i, 0)),
        out_shape=jax.ShapeDtypeStruct((e, r, d), f32),
        compiler_params=_cparams(("arbitrary", "arbitrary")),
        name="expert_ffn",
    )(xg, gv, wg, wu, wd)


def _rope_tables(n_tokens):
    rows = n_tokens // GRID_W
    row = jnp.repeat(jnp.arange(rows), GRID_W).astype(f32)
    col = jnp.tile(jnp.arange(GRID_W), rows).astype(f32)
    n_freq = HEAD_DIM // 4
    inv = ROPE_THETA ** (-jnp.arange(n_freq, dtype=f32) / n_freq)
    ang_r = row[:, None] * inv
    ang_c = col[:, None] * inv
    cos = jnp.concatenate([jnp.cos(ang_r), jnp.cos(ang_r), jnp.cos(ang_c), jnp.cos(ang_c)], axis=-1)
    sins = jnp.concatenate([-jnp.sin(ang_r), jnp.sin(ang_r), -jnp.sin(ang_c), jnp.sin(ang_c)], axis=-1)
    return cos, sins


def _dt_layout(v):
    lead = v.shape[:-1]
    v = v.reshape(lead + (2, SSD_GROUPS, SSD_KH))
    v = jnp.moveaxis(v, -2, -3).reshape(lead + (SSD_GROUPS, 2 * SSD_KH))
    v = jnp.pad(v, [(0, 0)] * (len(lead) + 1) + [(0, LANES - 2 * SSD_KH)])
    return v.reshape(lead + (DT_PAD,))


def _group_layout(v):
    outs = []
    for g in range(SSD_GROUPS):
        outs.append(jnp.concatenate([
            v[..., g * SSD_GW:(g + 1) * SSD_GW],
            v[..., SSD_W + g * SSD_STATE:SSD_W + (g + 1) * SSD_STATE],
            v[..., SSD_W + SSD_BC_W + g * SSD_STATE:SSD_W + SSD_BC_W + (g + 1) * SSD_STATE]], axis=-1))
    return jnp.stack(outs)


def _route(logits, b, t):
    cap = EC_FACTOR * t // N_EXPERTS
    aff = jax.nn.softmax(logits[:, :N_EXPERTS].reshape(b, t, N_EXPERTS), axis=-1)
    gval, idx = lax.top_k(jnp.swapaxes(aff, 1, 2), cap)
    rows = idx + (jnp.arange(b, dtype=idx.dtype) * t)[:, None, None]
    to_e = lambda u: jnp.swapaxes(u, 0, 1).reshape(N_EXPERTS, b * cap)
    return to_e(gval), to_e(rows)


def kernel(x, c, ctx, c_ctx, norm1_g, norm2_g, w_mod, b_mod, w_in, q_norm_g, k_norm_g, attn_sink, ssd_conv_w,
           ssd_conv_b, ssd_dt_bias, ssd_a_log, ssd_d, ssd_norm_g, sc_conv_w, w_out, w_router, w_expert_gate,
           w_expert_up, w_expert_down):
    b, s, d = x.shape
    lc = ctx.shape[1]
    nl = w_in.shape[0]
    cos, sins = _rope_tables(s)

    nrow = -(-(b + 1) // 8) * 8
    cvec = jnp.zeros((nrow, d), f32).at[:b].set(c).at[b].set(c_ctx)
    mods = _modulations(cvec, w_mod, b_mod).reshape(nl, nrow, N_MOD, 1, d)

    cuts = [0]
    for wdt in (ATT_W, ATT_KV_W, ATT_KV_W, SSD_W, SSD_XBC_W, SSD_DT_W, SC_W, SC_W, SC_W):
        cuts.append(cuts[-1] + wdt)
    w_dt = _dt_layout(w_in[:, :, cuts[5]:cuts[6]])
    w_in_p = jnp.concatenate([w_in[:, :, :cuts[5]], w_dt, w_in[:, :, cuts[6]:]], axis=-1).astype(bf16)
    w_out_b = w_out.astype(bf16)
    wr_pad = jnp.pad(w_router, ((0, 0), (0, 0), (0, ROUTER_PAD - N_EXPERTS)))
    wr3 = jnp.stack(_split3(wr_pad), axis=1)
    wg_b, wu_b, wd_b = w_expert_gate.astype(bf16), w_expert_up.astype(bf16), w_expert_down.astype(bf16)
    dtb = _dt_layout(ssd_dt_bias.reshape(nl, SSD_DT_W)).reshape(nl, SSD_GROUPS, 1, LANES)
    a_neg = _dt_layout(-jnp.exp(ssd_a_log.reshape(nl, SSD_DT_W))).reshape(nl, SSD_GROUPS, 1, LANES)
    dsk = jnp.repeat(ssd_d, SSD_HEAD_DIM, axis=-1).reshape(nl, 1, SSD_W)

    xl = x.reshape(b * s, d)
    xc = ctx.reshape(b * lc, d)
    for i in range(nl):
        with_ctx_out = i < nl - 1
        sh1, sc1, g1, sh2, sc2, g2 = (mods[i, :, j] for j in range(N_MOD))
        n1 = norm1_g[i].reshape(1, d)
        n2 = norm2_g[i].reshape(1, d)
        qg = q_norm_g[i].reshape(1, HEAD_DIM)
        kg = k_norm_g[i].reshape(1, HEAD_DIM)

        q, k, v, z, xbc, dtr, scb, scc, sch = _inproj(xl, sh1, sc1, n1, w_in_p[i], 512, s)
        qc, kc, vc, zc, xbcc, dtrc, scbc, sccc, schc = _inproj(xc, sh1[b:], sc1[b:], n1, w_in_p[i], lc, b * lc)

        kp = _kprep(k, kg, cos, sins, s, True)
        kcp = _kprep(kc, kg, cos, sins, lc, False)
        r3 = lambda u, t: u.reshape(b, t, u.shape[-1])
        att = _attention(r3(q, s), qg, attn_sink[i], r3(kcp, lc), r3(vc, lc), r3(kp, s), r3(v, s), cos, sins)

        ssd_c, ssd_l = _ssd(r3(xbcc, lc), r3(dtrc, lc), r3(zc, lc), r3(xbc, s), r3(dtr, s), r3(z, s),
                            _group_layout(ssd_conv_w[i]), _group_layout(ssd_conv_b[i].reshape(1, -1)),
                            dtb[i], a_neg[i], dsk[i], ssd_norm_g[i].reshape(1, SSD_W))
        sconv = _sconv(r3(scb, s), r3(scc, s), r3(sch, s), sc_conv_w[i])

        xm, h2, lg = _outproj(att.reshape(b * s, ATT_W), ssd_l.reshape(b * s, SSD_W), sconv.reshape(b * s, SC_W),
                              xl, g1, sh2, sc2, n2, w_out_b[i], wr3[i], 512, s)
        gv, rows = _route(lg, b, s)
        xg = h2[rows]
        if with_ctx_out:
            attc = _attention(r3(qc, lc), qg, attn_sink[i], r3(kcp, lc), r3(vc, lc))
            sconvc = _sconv(r3(scbc, lc), r3(sccc, lc), r3(schc, lc), sc_conv_w[i])
            xmc, h2c, lgc = _outproj(attc.reshape(b * lc, ATT_W), ssd_c.reshape(b * lc, SSD_W),
                                     sconvc.reshape(b * lc, SC_W), xc, g1[b:], sh2[b:], sc2[b:], n2,
                                     w_out_b[i], wr3[i], lc, b * lc)
            gvc, rowsc = _route(lgc, b, lc)
            xg = jnp.concatenate([xg, h2c[rowsc]], axis=1)
            gv_all = jnp.concatenate([gv, gvc], axis=1)
        else:
            gv_all = gv
        nrows = xg.shape[1]
        y = _expert_ffn(xg, gv_all[..., None], wg_b[i], wu_b[i], wd_b[i], _ffn_tile(nrows))
        nlat = rows.shape[1]
        moe = jnp.zeros((b * s, d), f32).at[rows.reshape(-1)].add(y[:, :nlat].reshape(-1, d))
        xl = (xm.reshape(b, s, d) + g2[:b] * moe.reshape(b, s, d)).reshape(b * s, d)
        if with_ctx_out:
            moec = jnp.zeros((b * lc, d), f32).at[rowsc.reshape(-1)].add(y[:, nlat:].reshape(-1, d))
            xc = xmc + g2[b, 0][None, :] * moec
    return xl.reshape(b, s, d)
```

```python
import functools

import jax
import jax.numpy as jnp
from jax import lax
from jax.experimental import pallas as pl
from jax.experimental.pallas import tpu as pltpu

f32 = jnp.float32
bf16 = jnp.bfloat16

D_MODEL = 2048
DEPTH = 4
GRID_W = 64
NORM_EPS = 1e-6
N_MOD = 6
HEAD_DIM = 128
ATT_HEADS = 8
ATT_KV_HEADS = 2
ATT_GROUP = ATT_HEADS // ATT_KV_HEADS
ATT_WINDOW = 128
ATT_BLOCK = 128
ROPE_THETA = 10000.0
SSD_HEAD_DIM = 64
SSD_W = 512
SSD_HEADS = 8
SSD_GROUPS = 2
SSD_KH = SSD_HEADS // SSD_GROUPS
SSD_STATE = 128
SSD_CHUNK = 128
SSD_GW = SSD_W // SSD_GROUPS
SC_W = 512
ATT_W = ATT_HEADS * HEAD_DIM
ATT_KV_W = ATT_KV_HEADS * HEAD_DIM
SSD_BC_W = SSD_GROUPS * SSD_STATE
SSD_XBC_W = SSD_W + 2 * SSD_BC_W
SSD_DT_W = 2 * SSD_HEADS
N_EXPERTS = 16
EC_FACTOR = 2
EXPERT_FF = 1024

LANES = 128
DT_PAD = SSD_GROUPS * LANES
ROUTER_PAD = LANES
SEG_W = (ATT_W, ATT_KV_W, ATT_KV_W, SSD_W, SSD_XBC_W, DT_PAD, SC_W, SC_W, SC_W)
SEG_DT = (bf16, bf16, bf16, bf16, bf16, f32, bf16, bf16, bf16)
IN_W_PAD = sum(SEG_W)
VMEM_LIMIT = 56 * 1024 * 1024
NEG_INF = float("-inf")
LOG2E = 1.4426950408889634
ATT_TQ = 512
COMBINE_TILE = 1024
DT_ROWS = 16
FFN_MAX_TILE = 576


def _cparams(sem):
    return pltpu.CompilerParams(dimension_semantics=sem, vmem_limit_bytes=VMEM_LIMIT)


def _silu(v):
    return v * jax.nn.sigmoid(v)


def _shifted_rows(pad_ref, r0, n):
    x0 = pad_ref[pl.ds(r0 + 8, n), :]
    before = pad_ref[pl.ds(r0, 8), :][7:8, :]
    after = pad_ref[pl.ds(r0 + 8 + n, 8), :][0:1, :]
    ri = lax.broadcasted_iota(jnp.int32, x0.shape, 0)
    xm = jnp.where(ri == 0, before, pltpu.roll(x0, 1, 0))
    xp = jnp.where(ri == n - 1, after, pltpu.roll(x0, n - 1, 0))
    return xm, x0, xp


def _split3(v):
    hi = v.astype(bf16)
    r = v - hi.astype(f32)
    mid = r.astype(bf16)
    lo = (r - mid.astype(f32)).astype(bf16)
    return hi, mid, lo


def _mod_kernel(c_ref, w_ref, b_ref, o_ref):
    a = _silu(c_ref[...]).astype(bf16)
    o_ref[0] = jnp.dot(a, w_ref[0].astype(bf16), preferred_element_type=f32) + b_ref[0]


def _modulations(cvec, w_mod, b_mod):
    nl, d, n = w_mod.shape
    r = cvec.shape[0]
    tn = 1024
    return pl.pallas_call(
        _mod_kernel,
        grid=(nl, n // tn),
        in_specs=[pl.BlockSpec((r, d), lambda l, j: (0, 0)),
                  pl.BlockSpec((1, d, tn), lambda l, j: (l, 0, j)),
                  pl.BlockSpec((1, 1, tn), lambda l, j: (l, 0, j))],
        out_specs=pl.BlockSpec((1, r, tn), lambda l, j: (l, 0, j)),
        out_shape=jax.ShapeDtypeStruct((nl, r, n), f32),
        compiler_params=_cparams(("arbitrary", "arbitrary")),
        name="modulations",
    )(cvec, w_mod, b_mod.reshape(nl, 1, n))


def _inproj_kernel(x_ref, sh_ref, sc_ref, g_ref, w_ref, *rest):
    outs, h_scr = rest[:-1], rest[-1]
    x = x_ref[...]
    y = x * lax.rsqrt(jnp.mean(x * x, axis=-1, keepdims=True) + NORM_EPS) * g_ref[...]
    h_scr[...] = (y * (1.0 + sc_ref[0]) + sh_ref[0]).astype(bf16)
    off = 0
    for ref, width in zip(outs, SEG_W):
        for c0 in range(0, width, 512):
            cw = min(512, width - c0)
            ref[:, c0:c0 + cw] = jnp.dot(h_scr[...], w_ref[0, :, off + c0:off + c0 + cw],
                                         preferred_element_type=f32).astype(ref.dtype)
        off += width


def _inproj(x2d, shift, scale, g, w, layer, tm, rows_per_mod):
    m, d = x2d.shape
    tpm = rows_per_mod // tm
    mod_spec = pl.BlockSpec((1, 1, d), lambda i: (i // tpm, 0, 0))
    return pl.pallas_call(
        _inproj_kernel,
        grid=(m // tm,),
        in_specs=[pl.BlockSpec((tm, d), lambda i: (i, 0)), mod_spec, mod_spec,
                  pl.BlockSpec((1, d), lambda i: (0, 0)),
                  pl.BlockSpec((1, d, IN_W_PAD), lambda i: (layer, 0, 0), pipeline_mode=pl.Buffered(1))],
        out_specs=[pl.BlockSpec((tm, wd), lambda i: (i, 0)) for wd in SEG_W],
        out_shape=[jax.ShapeDtypeStruct((m, wd), dt) for wd, dt in zip(SEG_W, SEG_DT)],
        scratch_shapes=[pltpu.VMEM((tm, d), bf16)],
        compiler_params=_cparams(("arbitrary",)),
        name="inproj",
    )(x2d, shift, scale, g, w)


def _norm_rope(v, g, cos, sins):
    y = v * lax.rsqrt(jnp.mean(v * v, axis=-1, keepdims=True) + NORM_EPS) * g
    if cos is None:
        return y
    lane = lax.broadcasted_iota(jnp.int32, y.shape, 1)
    quarter = HEAD_DIM // 4
    partner = jnp.where((lane % (2 * quarter)) < quarter,
                        pltpu.roll(y, HEAD_DIM - quarter, 1), pltpu.roll(y, quarter, 1))
    return y * cos + partner * sins


def _kprep_kernel(k_ref, g_ref, cos_ref, sin_ref, o_ref, *, rope):
    for hh in range(ATT_KV_HEADS):
        sl = slice(hh * HEAD_DIM, (hh + 1) * HEAD_DIM)
        v = k_ref[:, sl].astype(f32)
        o_ref[:, sl] = _norm_rope(v, g_ref[...], cos_ref[...] if rope else None,
                                  sin_ref[...] if rope else None).astype(bf16)


def _kprep(k2d, g, cos, sins, seq, rope):
    m = k2d.shape[0]
    tk = min(512, seq)
    nt = seq // tk
    tab = pl.BlockSpec((tk, HEAD_DIM), lambda i: (i % nt, 0))
    return pl.pallas_call(
        functools.partial(_kprep_kernel, rope=rope),
        grid=(m // tk,),
        in_specs=[pl.BlockSpec((tk, ATT_KV_W), lambda i: (i, 0)),
                  pl.BlockSpec((1, HEAD_DIM), lambda i: (0, 0)), tab, tab],
        out_specs=pl.BlockSpec((tk, ATT_KV_W), lambda i: (i, 0)),
        out_shape=jax.ShapeDtypeStruct((m, ATT_KV_W), bf16),
        compiler_params=_cparams(("arbitrary",)),
        name="kprep_rope" if rope else "kprep",
    )(k2d, g, cos, sins)


def _attn_kernel(*refs, band, tq, seq):
    if band:
        (sink_ref, q_ref, cos_ref, sin_ref, qg_ref, kp_ref, km_ref, kn_ref, vp_ref, vm_ref, vn_ref,
         kc_ref, vc_ref, o_ref) = refs
    else:
        sink_ref, q_ref, qg_ref, kc_ref, vc_ref, o_ref = refs
    h = pl.program_id(1)
    n = pl.program_id(2)
    qscale = HEAD_DIM ** -0.5 * LOG2E
    kc = kc_ref[0]
    vc = vc_ref[0]
    rows = ATT_GROUP * ATT_BLOCK
    if band:
        kwin = jnp.concatenate([kp_ref[0], km_ref[0], kn_ref[0]], axis=0)
        vwin = jnp.concatenate([vp_ref[0], vm_ref[0], vn_ref[0]], axis=0)
        ri = lax.broadcasted_iota(jnp.int32, (rows, 3 * ATT_BLOCK), 0) % ATT_BLOCK
        ci = lax.broadcasted_iota(jnp.int32, (rows, 3 * ATT_BLOCK), 1)
        band_bias = jnp.where(ci >= ri, jnp.where(ci <= ri + 2 * ATT_WINDOW, 0.0, NEG_INF), NEG_INF)
        col = lax.broadcasted_iota(jnp.int32, (1, 3 * ATT_BLOCK), 1)
    sinkcol = jnp.concatenate(
        [jnp.full((ATT_BLOCK, 1), sink_ref[h * ATT_GROUP + hh] * LOG2E, f32) for hh in range(ATT_GROUP)], axis=0)
    nt = (((1,), (1,)), ((), ()))
    for jb in range(tq // ATT_BLOCK):
        r0 = jb * ATT_BLOCK
        qparts = []
        for hh in range(ATT_GROUP):
            qv = q_ref[0, r0:r0 + ATT_BLOCK, hh * HEAD_DIM:(hh + 1) * HEAD_DIM].astype(f32)
            if band:
                qv = _norm_rope(qv, qg_ref[...], cos_ref[r0:r0 + ATT_BLOCK, :], sin_ref[r0:r0 + ATT_BLOCK, :])
            else:
                qv = _norm_rope(qv, qg_ref[...], None, None)
            qparts.append((qv * qscale).astype(bf16))
        qs = jnp.concatenate(qparts, axis=0)
        s_c = lax.dot_general(qs, kc, nt, preferred_element_type=f32)
        m = jnp.maximum(jnp.max(s_c, axis=-1, keepdims=True), sinkcol)
        if band:
            kb = kwin[r0:r0 + 3 * ATT_BLOCK]
            vb = vwin[r0:r0 + 3 * ATT_BLOCK]
            kpos = col + (n * tq + r0 - ATT_BLOCK)
            col_bias = jnp.where(kpos >= 0, jnp.where(kpos < seq, 0.0, NEG_INF), NEG_INF)
            s_w = lax.dot_general(qs, kb, nt, preferred_element_type=f32) + (band_bias + col_bias)
            m = jnp.maximum(m, jnp.max(s_w, axis=-1, keepdims=True))
            p_w = jnp.exp2(s_w - m)
        p_c = jnp.exp2(s_c - m)
        den = jnp.sum(p_c, axis=-1, keepdims=True) + jnp.exp2(sinkcol - m)
        o = jnp.dot(p_c.astype(bf16), vc, preferred_element_type=f32)
        if band:
            den = den + jnp.sum(p_w, axis=-1, keepdims=True)
            o = o + jnp.dot(p_w.astype(bf16), vb, preferred_element_type=f32)
        o = o / den
        for hh in range(ATT_GROUP):
            o_ref[0, r0:r0 + ATT_BLOCK, hh * HEAD_DIM:(hh + 1) * HEAD_DIM] = (
                o[hh * ATT_BLOCK:(hh + 1) * ATT_BLOCK].astype(bf16))


def _attention(q, qg, sink, kc, vc, k=None, v=None, cos=None, sins=None):
    b, t, _ = q.shape
    lc = kc.shape[1]
    band = k is not None
    tq = ATT_TQ if t % ATT_TQ == 0 else 2 * ATT_BLOCK
    gw = ATT_GROUP * HEAD_DIM
    nblk = t // ATT_BLOCK
    per = tq // ATT_BLOCK
    smem = pl.BlockSpec(memory_space=pltpu.SMEM)
    qspec = pl.BlockSpec((1, tq, gw), lambda bi, h, n: (bi, n, h))
    gspec = pl.BlockSpec((1, HEAD_DIM), lambda bi, h, n: (0, 0))
    cspec = pl.BlockSpec((1, lc, HEAD_DIM), lambda bi, h, n: (bi, 0, h))
    if band:
        tab = pl.BlockSpec((tq, HEAD_DIM), lambda bi, h, n: (n, 0))
        prev = pl.BlockSpec((1, ATT_BLOCK, HEAD_DIM), lambda bi, h, n: (bi, jnp.maximum(n * per - 1, 0), h))
        main = pl.BlockSpec((1, tq, HEAD_DIM), lambda bi, h, n: (bi, n, h))
        nxt = pl.BlockSpec((1, ATT_BLOCK, HEAD_DIM), lambda bi, h, n: (bi, jnp.minimum((n + 1) * per, nblk - 1), h))
        in_specs = [smem, qspec, tab, tab, gspec, prev, main, nxt, prev, main, nxt, cspec, cspec]
        args = (sink, q, cos, sins, qg, k, k, k, v, v, v, kc, vc)
    else:
        in_specs = [smem, qspec, gspec, cspec, cspec]
        args = (sink, q, qg, kc, vc)
    return pl.pallas_call(
        functools.partial(_attn_kernel, band=band, tq=tq, seq=t),
        grid=(b, ATT_KV_HEADS, t // tq),
        in_specs=in_specs,
        out_specs=pl.BlockSpec((1, tq, gw), lambda bi, h, n: (bi, n, h)),
        out_shape=jax.ShapeDtypeStruct((b, t, ATT_W), bf16),
        compiler_params=_cparams(("arbitrary", "arbitrary", "arbitrary")),
        name="attn_band" if band else "attn_ctx",
    )(*args)


def _ssd_kernel(xc_ref, bc_ref, cc_ref, dtc_ref, zc_ref, xl_ref, bl_ref, cl_ref, dtl_ref, zl_ref,
                cw_ref, cb_ref, dtb_ref, a_ref, acol_ref, dsk_ref, ng_ref, oc_ref, ol_ref,
                pad_scr, xs_scr, cm_scr, bt_scr, dts_scr, dtt_scr, y_scr, h_scr, *, lc, seq):
    ck = SSD_CHUNK
    gw = SSD_GW
    nst = SSD_STATE
    row = lax.broadcasted_iota(jnp.int32, (ck, ck), 0)
    col = lax.broadcasted_iota(jnp.int32, (ck, ck), 1)
    tri = (row >= col, col >= row)
    tri_bf = tuple(jnp.where(t, 1.0, 0.0).astype(bf16) for t in tri)
    first = col < SSD_HEAD_DIM
    h_scr[...] = jnp.zeros(h_scr.shape, f32)

    def run_seq(t, x_ref, b_ref, c_ref, dt_ref, z_ref, o_ref):
        nc = t // ck
        zero8 = jnp.zeros((8, gw + 2 * nst), f32)
        pad_scr[0:8, :] = zero8
        pad_scr[8 + t:16 + t, :] = zero8

        def fill(c, carry):
            r0 = pl.multiple_of(c * ck, ck)
            pad_scr[pl.ds(r0 + 8, ck), 0:gw] = x_ref[0, pl.ds(r0, ck), :].astype(f32)
            pad_scr[pl.ds(r0 + 8, ck), gw:gw + nst] = b_ref[0, pl.ds(r0, ck), :].astype(f32)
            pad_scr[pl.ds(r0 + 8, ck), gw + nst:gw + 2 * nst] = c_ref[0, pl.ds(r0, ck), :].astype(f32)
            return carry

        lax.fori_loop(0, nc, fill, 0)

        def conv(c, carry):
            r0 = pl.multiple_of(c * ck, ck)
            xm, x0, xp = _shifted_rows(pad_scr, r0, ck)
            act = _silu(cw_ref[0, 0:1, :] * xm + cw_ref[0, 1:2, :] * x0 + cw_ref[0, 2:3, :] * xp + cb_ref[0])
            xs = act[:, 0:gw]
            y_scr[pl.ds(r0, ck), :] = dsk_ref[...] * xs
            xs_scr[pl.ds(r0, ck), :] = xs.astype(bf16)
            bt_scr[c] = act[:, gw:gw + nst].T
            cm_scr[pl.ds(r0, ck), :] = act[:, gw + nst:gw + 2 * nst].astype(bf16)
            dv = dt_ref[0, pl.ds(r0, ck), :] + dtb_ref[0]
            dts = jnp.maximum(dv, 0.0) + jnp.log1p(jnp.exp(-jnp.abs(dv)))
            dts_scr[pl.ds(r0, ck), :] = dts
            dtt_scr[c] = dts.T[0:DT_ROWS, :]
            return carry

        lax.fori_loop(0, nc, conv, 0)

        def chunk(i, carry):
            for d in range(2):
                c = i if d == 0 else nc - 1 - i
                r0 = pl.multiple_of(c * ck, ck)
                cb16 = cm_scr[pl.ds(r0, ck), :]
                bt32 = bt_scr[c]
                dt = dts_scr[pl.ds(r0, ck), :]
                dt_t = dtt_scr[c]
                hi, mid, lo = _split3(dt * a_ref[0])
                cum = (jnp.dot(tri_bf[d], hi, preferred_element_type=f32)
                       + jnp.dot(tri_bf[d], mid, preferred_element_type=f32)
                       + jnp.dot(tri_bf[d], lo, preferred_element_type=f32))
                hi_t, mid_t, lo_t = _split3(dt_t * acol_ref[0, 0])
                cum_t = (jnp.dot(hi_t, tri_bf[1 - d], preferred_element_type=f32)
                         + jnp.dot(mid_t, tri_bf[1 - d], preferred_element_type=f32)
                         + jnp.dot(lo_t, tri_bf[1 - d], preferred_element_type=f32))
                tot = cum[ck - 1:ck, :] if d == 0 else cum[0:1, :]
                tot_t = cum_t[:, ck - 1:ck] if d == 0 else cum_t[:, 0:1]
                w_t = dt_t * jnp.exp(tot_t - cum_t)
                etot = jnp.exp(tot)
                cbm = jnp.dot(cb16, bt32.astype(bf16), preferred_element_type=f32)
                for p in range(SSD_KH // 2):
                    mks, ecols, sts = [], [], []
                    xp16 = xs_scr[pl.ds(r0, ck), p * LANES:(p + 1) * LANES]
                    for k in (2 * p, 2 * p + 1):
                        j = d * SSD_KH + k
                        ccol = jnp.broadcast_to(cum[:, j:j + 1], (ck, ck))
                        decay = jnp.exp(jnp.where(tri[d], ccol - cum_t[j:j + 1, :], NEG_INF))
                        mks.append((cbm * decay * dt_t[j:j + 1, :]).astype(bf16))
                        ecols.append(jnp.exp(ccol))
                        sts.append(jnp.dot((bt32 * w_t[j:j + 1, :]).astype(bf16), xp16, preferred_element_type=f32))
                    j0 = d * SSD_KH + 2 * p
                    hp = h_scr[d * (SSD_KH // 2) + p]
                    ydiag = jnp.where(first, jnp.dot(mks[0], xp16, preferred_element_type=f32),
                                      jnp.dot(mks[1], xp16, preferred_element_type=f32))
                    yoff = jnp.where(first, ecols[0], ecols[1]) * jnp.dot(cb16, hp.astype(bf16),
                                                                          preferred_element_type=f32)
                    ysl = (pl.ds(r0, ck), slice(p * LANES, (p + 1) * LANES))
                    y_scr[ysl] = y_scr[ysl] + (ydiag + yoff)
                    erow = jnp.where(first[0:1, :], etot[:, j0:j0 + 1], etot[:, j0 + 1:j0 + 2])
                    h_scr[d * (SSD_KH // 2) + p] = hp * erow + jnp.where(first, sts[0], sts[1])
            return carry

        lax.fori_loop(0, nc, chunk, 0)

        def gate(c, carry):
            r0 = pl.multiple_of(c * ck, ck)
            u = y_scr[pl.ds(r0, ck), :] * _silu(z_ref[0, pl.ds(r0, ck), :].astype(f32))
            u = u * lax.rsqrt(jnp.mean(u * u, axis=-1, keepdims=True) + NORM_EPS)
            o_ref[0, pl.ds(r0, ck), :] = (u * ng_ref[...]).astype(bf16)
            return carry

        lax.fori_loop(0, nc, gate, 0)

    run_seq(lc, xc_ref, bc_ref, cc_ref, dtc_ref, zc_ref, oc_ref)
    run_seq(seq, xl_ref, bl_ref, cl_ref, dtl_ref, zl_ref, ol_ref)


def _ssd(xbc_c, dt_c, z_c, xbc_l, dt_l, z_l, cw, cb, dtb, a_neg, a_col, dsk, ng):
    b, lc, _ = xbc_c.shape
    seq = xbc_l.shape[1]
    gw, nst = SSD_GW, SSD_STATE
    xoff = SSD_W // nst

    def seq_specs(t):
        return [pl.BlockSpec((1, t, gw), lambda bi, g: (bi, 0, g)),
                pl.BlockSpec((1, t, nst), lambda bi, g: (bi, 0, xoff + g)),
                pl.BlockSpec((1, t, nst), lambda bi, g: (bi, 0, xoff + SSD_GROUPS + g)),
                pl.BlockSpec((1, t, LANES), lambda bi, g: (bi, 0, g)),
                pl.BlockSpec((1, t, gw), lambda bi, g: (bi, 0, g))]

    cwid = gw + 2 * nst
    par_specs = [pl.BlockSpec((1, 3, cwid), lambda bi, g: (g, 0, 0)),
                 pl.BlockSpec((1, 1, cwid), lambda bi, g: (g, 0, 0)),
                 pl.BlockSpec((1, 1, LANES), lambda bi, g: (g, 0, 0)),
                 pl.BlockSpec((1, 1, LANES), lambda bi, g: (g, 0, 0)),
                 pl.BlockSpec((1, 1, DT_ROWS, 1), lambda bi, g: (g, 0, 0, 0)),
                 pl.BlockSpec((1, gw), lambda bi, g: (0, g)),
                 pl.BlockSpec((1, gw), lambda bi, g: (0, g))]
    nck = seq // SSD_CHUNK
    return pl.pallas_call(
        functools.partial(_ssd_kernel, lc=lc, seq=seq),
        grid=(b, SSD_GROUPS),
        in_specs=seq_specs(lc) + seq_specs(seq) + par_specs,
        out_specs=[pl.BlockSpec((1, lc, gw), lambda bi, g: (bi, 0, g)),
                   pl.BlockSpec((1, seq, gw), lambda bi, g: (bi, 0, g))],
        out_shape=[jax.ShapeDtypeStruct((b, lc, SSD_W), bf16), jax.ShapeDtypeStruct((b, seq, SSD_W), bf16)],
        scratch_shapes=[pltpu.VMEM((seq + 16, cwid), f32),
                        pltpu.VMEM((seq, gw), bf16),
                        pltpu.VMEM((seq, nst), bf16),
                        pltpu.VMEM((nck, nst, SSD_CHUNK), f32),
                        pltpu.VMEM((seq, LANES), f32),
                        pltpu.VMEM((nck, DT_ROWS, SSD_CHUNK), f32),
                        pltpu.VMEM((seq, gw), f32),
                        pltpu.VMEM((SSD_KH, nst, LANES), f32)],
        compiler_params=_cparams(("arbitrary", "arbitrary")),
        name="ssd",
    )(xbc_c, xbc_c, xbc_c, dt_c, z_c, xbc_l, xbc_l, xbc_l, dt_l, z_l, cw, cb, dtb, a_neg, a_col, dsk, ng)


def _sconv_kernel(b_ref, c_ref, h_ref, w_ref, o_ref, pad_scr, *, t):
    ck = min(256, t)
    zero8 = jnp.zeros((8, LANES), f32)
    pad_scr[0:8, :] = zero8
    pad_scr[8 + t:16 + t, :] = zero8

    def fill(c, carry):
        r0 = pl.multiple_of(c * ck, ck)
        pad_scr[pl.ds(r0 + 8, ck), :] = c_ref[0, pl.ds(r0, ck), :].astype(f32) * h_ref[0, pl.ds(r0, ck), :].astype(f32)
        return carry

    lax.fori_loop(0, t // ck, fill, 0)

    def conv(c, carry):
        r0 = pl.multiple_of(c * ck, ck)
        xm, x0, xp = _shifted_rows(pad_scr, r0, ck)
        acc = w_ref[0:1, :] * xm + w_ref[1:2, :] * x0 + w_ref[2:3, :] * xp
        o_ref[0, pl.ds(r0, ck), :] = (b_ref[0, pl.ds(r0, ck), :].astype(f32) * acc).astype(bf16)
        return carry

    lax.fori_loop(0, t // ck, conv, 0)


def _sconv(scb, scc, sch, w):
    b, t, cw = scb.shape
    spec = pl.BlockSpec((1, t, LANES), lambda bi, j: (bi, 0, j))
    return pl.pallas_call(
        functools.partial(_sconv_kernel, t=t),
        grid=(b, cw // LANES),
        in_specs=[spec, spec, spec, pl.BlockSpec((3, LANES), lambda bi, j: (0, j))],
        out_specs=spec,
        out_shape=jax.ShapeDtypeStruct((b, t, cw), bf16),
        scratch_shapes=[pltpu.VMEM((t + 16, LANES), f32)],
        compiler_params=_cparams(("arbitrary", "arbitrary")),
        name="sconv",
    )(scb, scc, sch, w)


def _outproj_kernel(att_ref, ssd_ref, sc_ref, x_ref, g1_ref, sh2_ref, sc2_ref, n2_ref, wo_ref, wr_ref,
                    xo_ref, h2_ref, lg_ref, *, sub):
    for r0 in range(0, x_ref.shape[0], sub):
        rs = slice(r0, r0 + sub)
        acc = jnp.dot(att_ref[rs, :], wo_ref[0, 0:ATT_W, :], preferred_element_type=f32)
        acc = acc + jnp.dot(ssd_ref[rs, :], wo_ref[0, ATT_W:ATT_W + SSD_W, :], preferred_element_type=f32)
        acc = acc + jnp.dot(sc_ref[rs, :], wo_ref[0, ATT_W + SSD_W:, :], preferred_element_type=f32)
        x = x_ref[rs, :] + g1_ref[0] * acc
        xo_ref[rs, :] = x
        y = x * lax.rsqrt(jnp.mean(x * x, axis=-1, keepdims=True) + NORM_EPS) * n2_ref[...]
        h2 = y * (1.0 + sc2_ref[0]) + sh2_ref[0]
        hh = h2.astype(bf16)
        h2_ref[rs, :] = hh
        hm = (h2 - hh.astype(f32)).astype(bf16)
        r1 = jnp.dot(hh, wr_ref[0], preferred_element_type=f32)
        r2 = jnp.dot(hm, wr_ref[0, :, 0:ROUTER_PAD], preferred_element_type=f32)
        lg_ref[rs, :] = r1[:, 0:ROUTER_PAD] + r1[:, ROUTER_PAD:] + r2


def _outproj(att, ssd, sconv, x2d, g1, sh2, sc2, n2, wo, wr2, layer, tm, rows_per_mod):
    m, d = x2d.shape
    tpm = rows_per_mod // tm
    mod_spec = pl.BlockSpec((1, 1, d), lambda i: (i // tpm, 0, 0))
    return pl.pallas_call(
        functools.partial(_outproj_kernel, sub=min(256, tm)),
        grid=(m // tm,),
        in_specs=[pl.BlockSpec((tm, ATT_W), lambda i: (i, 0)), pl.BlockSpec((tm, SSD_W), lambda i: (i, 0)),
                  pl.BlockSpec((tm, SC_W), lambda i: (i, 0)), pl.BlockSpec((tm, d), lambda i: (i, 0)),
                  mod_spec, mod_spec, mod_spec, pl.BlockSpec((1, d), lambda i: (0, 0)),
                  pl.BlockSpec((1, d, d), lambda i: (layer, 0, 0)),
                  pl.BlockSpec((1, d, 2 * ROUTER_PAD), lambda i: (layer, 0, 0))],
        out_specs=[pl.BlockSpec((tm, d), lambda i: (i, 0)), pl.BlockSpec((tm, d), lambda i: (i, 0)),
                   pl.BlockSpec((tm, ROUTER_PAD), lambda i: (i, 0))],
        out_shape=[jax.ShapeDtypeStruct((m, d), f32), jax.ShapeDtypeStruct((m, d), bf16),
                   jax.ShapeDtypeStruct((m, ROUTER_PAD), f32)],
        compiler_params=_cparams(("arbitrary",)),
        name="outproj",
    )(att, ssd, sconv, x2d, g1, sh2, sc2, n2, wo, wr2)


def _ffn_kernel(x_ref, gv_ref, wg_ref, wu_ref, wd_ref, o_ref):
    x = x_ref[0]
    a = jnp.dot(x, wg_ref[0, 0], preferred_element_type=f32)
    u = jnp.dot(x, wu_ref[0, 0], preferred_element_type=f32)
    hmid = (_silu(a) * u).astype(bf16)
    o_ref[0] = (jnp.dot(hmid, wd_ref[0, 0], preferred_element_type=f32) * gv_ref[0]).astype(bf16)


def _ffn_tile(nrows):
    for nt in range(1, nrows + 1):
        if nrows % nt == 0 and (nrows // nt) % 16 == 0 and nrows // nt <= FFN_MAX_TILE:
            return nrows // nt
    raise ValueError(f"no aligned row tile for {nrows} gathered rows")


def _expert_ffn(xg, gv, wg, wu, wd, layer):
    e, r, d = xg.shape
    ff = wg.shape[3]
    tm = _ffn_tile(r)
    return pl.pallas_call(
        _ffn_kernel,
        grid=(e, r // tm),
        in_specs=[pl.BlockSpec((1, tm, d), lambda ei, i: (ei, i, 0)),
                  pl.BlockSpec((1, tm, 1), lambda ei, i: (ei, i, 0)),
                  pl.BlockSpec((1, 1, d, ff), lambda ei, i: (layer, ei, 0, 0)),
                  pl.BlockSpec((1, 1, d, ff), lambda ei, i: (layer, ei, 0, 0)),
                  pl.BlockSpec((1, 1, ff, d), lambda ei, i: (layer, ei, 0, 0))],
        out_specs=pl.BlockSpec((1, tm, d), lambda ei, i: (ei, i, 0)),
        out_shape=jax.ShapeDtypeStruct((e, r, d), bf16),
        compiler_params=_cparams(("arbitrary", "arbitrary")),
        name="expert_ffn",
    )(xg, gv, wg, wu, wd)


def _combine_kernel(idx_ref, y_ref, xm_ref, g2_ref, o_ref, acc_scr):
    k = pl.program_id(1)
    e = pl.program_id(2)
    tt = acc_scr.shape[0]
    cap = y_ref.shape[1]

    @pl.when(e == 0)
    def _():
        acc_scr[...] = jnp.zeros(acc_scr.shape, f32)

    tok = lax.broadcasted_iota(jnp.int32, (tt, cap), 0) + k * tt
    onehot = jnp.where(tok == idx_ref[0, 0], 1.0, 0.0).astype(bf16)
    acc_scr[...] += jnp.dot(onehot, y_ref[0], preferred_element_type=f32)

    @pl.when(e == pl.num_programs(2) - 1)
    def _():
        o_ref[...] = xm_ref[...] + g2_ref[0] * acc_scr[...]


def _combine(idx, y, xm, g2, b, t, per_sample_gate):
    e, _, _, cap = idx.shape
    d = xm.shape[1]
    tt = min(COMBINE_TILE, t)
    nk = t // tt
    return pl.pallas_call(
        _combine_kernel,
        grid=(b, nk, e),
        in_specs=[pl.BlockSpec((1, 1, 1, cap), lambda bi, k, ei: (ei, bi, 0, 0)),
                  pl.BlockSpec((1, cap, d), lambda bi, k, ei: (ei, bi, 0)),
                  pl.BlockSpec((tt, d), lambda bi, k, ei: (bi * nk + k, 0)),
                  pl.BlockSpec((1, 1, d), lambda bi, k, ei: (bi if per_sample_gate else 0, 0, 0))],
        out_specs=pl.BlockSpec((tt, d), lambda bi, k, ei: (bi * nk + k, 0)),
        out_shape=jax.ShapeDtypeStruct(xm.shape, f32),
        scratch_shapes=[pltpu.VMEM((tt, d), f32)],
        compiler_params=_cparams(("arbitrary", "arbitrary", "arbitrary")),
        name="moe_combine",
    )(idx, y, xm, g2)


def _rope_tables(n_tokens):
    rows = n_tokens // GRID_W
    row = jnp.repeat(jnp.arange(rows), GRID_W).astype(f32)
    col = jnp.tile(jnp.arange(GRID_W), rows).astype(f32)
    n_freq = HEAD_DIM // 4
    inv = ROPE_THETA ** (-jnp.arange(n_freq, dtype=f32) / n_freq)
    ang_r = row[:, None] * inv
    ang_c = col[:, None] * inv
    cos = jnp.concatenate([jnp.cos(ang_r), jnp.cos(ang_r), jnp.cos(ang_c), jnp.cos(ang_c)], axis=-1)
    sins = jnp.concatenate([-jnp.sin(ang_r), jnp.sin(ang_r), -jnp.sin(ang_c), jnp.sin(ang_c)], axis=-1)
    return cos, sins


def _dt_layout(v):
    lead = v.shape[:-1]
    v = v.reshape(lead + (2, SSD_GROUPS, SSD_KH))
    v = jnp.moveaxis(v, -2, -3).reshape(lead + (SSD_GROUPS, 2 * SSD_KH))
    v = jnp.pad(v, [(0, 0)] * (len(lead) + 1) + [(0, LANES - 2 * SSD_KH)])
    return v.reshape(lead + (DT_PAD,))


def _group_layout(v):
    outs = []
    for g in range(SSD_GROUPS):
        outs.append(jnp.concatenate([
            v[..., g * SSD_GW:(g + 1) * SSD_GW],
            v[..., SSD_W + g * SSD_STATE:SSD_W + (g + 1) * SSD_STATE],
            v[..., SSD_W + SSD_BC_W + g * SSD_STATE:SSD_W + SSD_BC_W + (g + 1) * SSD_STATE]], axis=-1))
    return jnp.stack(outs)


def _route(logits, b, t):
    cap = EC_FACTOR * t // N_EXPERTS
    aff = jax.nn.softmax(logits[:, :N_EXPERTS].reshape(b, t, N_EXPERTS), axis=-1)
    gval, idx = lax.top_k(jnp.swapaxes(aff, 1, 2), cap)
    rows = idx + (jnp.arange(b, dtype=idx.dtype) * t)[:, None, None]
    to_e = lambda u: jnp.swapaxes(u, 0, 1).reshape(N_EXPERTS, b * cap)
    return to_e(gval)[..., None], to_e(rows), jnp.swapaxes(idx, 0, 1).reshape(N_EXPERTS, b, 1, cap)


def kernel(x, c, ctx, c_ctx, norm1_g, norm2_g, w_mod, b_mod, w_in, q_norm_g, k_norm_g, attn_sink, ssd_conv_w,
           ssd_conv_b, ssd_dt_bias, ssd_a_log, ssd_d, ssd_norm_g, sc_conv_w, w_out, w_router, w_expert_gate,
           w_expert_up, w_expert_down):
    b, s, d = x.shape
    lc = ctx.shape[1]
    nl = w_in.shape[0]
    cos, sins = _rope_tables(s)

    nrow = -(-(b + 1) // 8) * 8
    cvec = jnp.zeros((nrow, d), f32).at[:b].set(c).at[b].set(c_ctx)
    mods = _modulations(cvec, w_mod, b_mod).reshape(nl, nrow, N_MOD, 1, d)

    cuts = [0]
    for wdt in (ATT_W, ATT_KV_W, ATT_KV_W, SSD_W, SSD_XBC_W, SSD_DT_W, SC_W, SC_W, SC_W):
        cuts.append(cuts[-1] + wdt)
    w_dt = _dt_layout(w_in[:, :, cuts[5]:cuts[6]])
    w_in_p = jnp.concatenate([w_in[:, :, :cuts[5]], w_dt, w_in[:, :, cuts[6]:]], axis=-1).astype(bf16)
    w_out_b = w_out.astype(bf16)
    wr_pad = jnp.pad(w_router, ((0, 0), (0, 0), (0, ROUTER_PAD - N_EXPERTS)))
    wr_hi, wr_mid, _ = _split3(wr_pad)
    wr2 = jnp.concatenate([wr_hi, wr_mid], axis=-1)
    wg_b, wu_b, wd_b = w_expert_gate.astype(bf16), w_expert_up.astype(bf16), w_expert_down.astype(bf16)
    dtb = _dt_layout(ssd_dt_bias.reshape(nl, SSD_DT_W)).reshape(nl, SSD_GROUPS, 1, LANES)
    a_neg = _dt_layout(-jnp.exp(ssd_a_log.reshape(nl, SSD_DT_W))).reshape(nl, SSD_GROUPS, 1, LANES)
    a_col = a_neg[..., :DT_ROWS].reshape(nl, SSD_GROUPS, 1, DT_ROWS, 1)
    dsk = jnp.repeat(ssd_d, SSD_HEAD_DIM, axis=-1).reshape(nl, 1, SSD_W)

    xl = x.reshape(b * s, d)
    xc = ctx.reshape(b * lc, d)
    for i in range(nl):
        with_ctx_out = i < nl - 1
        sh1, sc1, g1, sh2, sc2, g2 = (mods[i, :, j] for j in range(N_MOD))
        n1 = norm1_g[i].reshape(1, d)
        n2 = norm2_g[i].reshape(1, d)
        qg = q_norm_g[i].reshape(1, HEAD_DIM)
        kg = k_norm_g[i].reshape(1, HEAD_DIM)

        q, k, v, z, xbc, dtr, scb, scc, sch = _inproj(xl, sh1, sc1, n1, w_in_p, i, 512, s)
        qc, kc, vc, zc, xbcc, dtrc, scbc, sccc, schc = _inproj(xc, sh1[b:], sc1[b:], n1, w_in_p, i, lc, b * lc)

        kp = _kprep(k, kg, cos, sins, s, True)
        kcp = _kprep(kc, kg, cos, sins, lc, False)
        r3 = lambda u, t: u.reshape(b, t, u.shape[-1])
        att = _attention(r3(q, s), qg, attn_sink[i], r3(kcp, lc), r3(vc, lc), r3(kp, s), r3(v, s), cos, sins)

        ssd_c, ssd_l = _ssd(r3(xbcc, lc), r3(dtrc, lc), r3(zc, lc), r3(xbc, s), r3(dtr, s), r3(z, s),
                            _group_layout(ssd_conv_w[i]), _group_layout(ssd_conv_b[i].reshape(1, -1)),
                            dtb[i], a_neg[i], a_col[i], dsk[i], ssd_norm_g[i].reshape(1, SSD_W))
        sconv = _sconv(r3(scb, s), r3(scc, s), r3(sch, s), sc_conv_w[i])

        xm, h2, lg = _outproj(att.reshape(b * s, ATT_W), ssd_l.reshape(b * s, SSD_W), sconv.reshape(b * s, SC_W),
                              xl, g1, sh2, sc2, n2, w_out_b, wr2, i, 512, s)
        gv, rows, idx = _route(lg, b, s)
        y = _expert_ffn(h2[rows], gv, wg_b, wu_b, wd_b, i)
        xl = _combine(idx, y, xm, g2, b, s, True)
        if with_ctx_out:
            attc = _attention(r3(qc, lc), qg, attn_sink[i], r3(kcp, lc), r3(vc, lc))
            sconvc = _sconv(r3(scbc, lc), r3(sccc, lc), r3(schc, lc), sc_conv_w[i])
            xmc, h2c, lgc = _outproj(attc.reshape(b * lc, ATT_W), ssd_c.reshape(b * lc, SSD_W),
                                     sconvc.reshape(b * lc, SC_W), xc, g1[b:], sh2[b:], sc2[b:], n2,
                                     w_out_b, wr2, i, lc, b * lc)
            gvc, rowsc, idxc = _route(lgc, b, lc)
            yc = _expert_ffn(h2c[rowsc], gvc, wg_b, wu_b, wd_b, i)
            xc = _combine(idxc, yc, xmc, g2[b:], b, lc, False)
    return xl.reshape(b, s, d)
```

```python
import functools

import jax
import jax.numpy as jnp
from jax import lax
from jax.experimental import pallas as pl
from jax.experimental.pallas import tpu as pltpu

f32 = jnp.float32
bf16 = jnp.bfloat16

D_MODEL = 2048
DEPTH = 4
GRID_W = 64
NORM_EPS = 1e-6
N_MOD = 6
HEAD_DIM = 128
ATT_HEADS = 8
ATT_KV_HEADS = 2
ATT_GROUP = ATT_HEADS // ATT_KV_HEADS
ATT_WINDOW = 128
ATT_BLOCK = 128
ROPE_THETA = 10000.0
SSD_HEAD_DIM = 64
SSD_W = 512
SSD_HEADS = 8
SSD_GROUPS = 2
SSD_KH = SSD_HEADS // SSD_GROUPS
SSD_STATE = 128
SSD_CHUNK = 128
SSD_GW = SSD_W // SSD_GROUPS
SC_W = 512
ATT_W = ATT_HEADS * HEAD_DIM
ATT_KV_W = ATT_KV_HEADS * HEAD_DIM
SSD_BC_W = SSD_GROUPS * SSD_STATE
SSD_XBC_W = SSD_W + 2 * SSD_BC_W
SSD_DT_W = 2 * SSD_HEADS
N_EXPERTS = 16
EC_FACTOR = 2
EXPERT_FF = 1024

LANES = 128
DT_PAD = SSD_GROUPS * LANES
ROUTER_PAD = LANES
SEG_W = (ATT_W, ATT_KV_W, ATT_KV_W, SSD_W, SSD_XBC_W, DT_PAD, SC_W, SC_W, SC_W)
SEG_DT = (bf16, bf16, bf16, bf16, bf16, f32, bf16, bf16, bf16)
IN_W_PAD = sum(SEG_W)
VMEM_LIMIT = 56 * 1024 * 1024
NEG_INF = float("-inf")
LOG2E = 1.4426950408889634
ATT_TQ = 512
COMBINE_TILE = 1024
COMBINE_SLOTS = 256
DT_ROWS = 16
FFN_MAX_TILE = 576


def _cparams(sem):
    return pltpu.CompilerParams(dimension_semantics=sem, vmem_limit_bytes=VMEM_LIMIT)


def _silu(v):
    return v * jax.nn.sigmoid(v)


def _shifted_rows(pad_ref, r0, n):
    x0 = pad_ref[pl.ds(r0 + 8, n), :]
    before = pad_ref[pl.ds(r0, 8), :][7:8, :]
    after = pad_ref[pl.ds(r0 + 8 + n, 8), :][0:1, :]
    ri = lax.broadcasted_iota(jnp.int32, x0.shape, 0)
    xm = jnp.where(ri == 0, before, pltpu.roll(x0, 1, 0))
    xp = jnp.where(ri == n - 1, after, pltpu.roll(x0, n - 1, 0))
    return xm, x0, xp


def _split3(v):
    hi = v.astype(bf16)
    r = v - hi.astype(f32)
    mid = r.astype(bf16)
    lo = (r - mid.astype(f32)).astype(bf16)
    return hi, mid, lo


def _mod_kernel(c_ref, w_ref, b_ref, o_ref):
    a = _silu(c_ref[...]).astype(bf16)
    o_ref[0] = jnp.dot(a, w_ref[0].astype(bf16), preferred_element_type=f32) + b_ref[0]


def _modulations(cvec, w_mod, b_mod):
    nl, d, n = w_mod.shape
    r = cvec.shape[0]
    tn = 1024
    return pl.pallas_call(
        _mod_kernel,
        grid=(nl, n // tn),
        in_specs=[pl.BlockSpec((r, d), lambda l, j: (0, 0)),
                  pl.BlockSpec((1, d, tn), lambda l, j: (l, 0, j)),
                  pl.BlockSpec((1, 1, tn), lambda l, j: (l, 0, j))],
        out_specs=pl.BlockSpec((1, r, tn), lambda l, j: (l, 0, j)),
        out_shape=jax.ShapeDtypeStruct((nl, r, n), f32),
        compiler_params=_cparams(("arbitrary", "arbitrary")),
        name="modulations",
    )(cvec, w_mod, b_mod.reshape(nl, 1, n))


def _inproj_kernel(x_ref, sh_ref, sc_ref, g_ref, w_ref, *rest):
    outs, h_scr = rest[:-1], rest[-1]
    x = x_ref[...]
    y = x * lax.rsqrt(jnp.mean(x * x, axis=-1, keepdims=True) + NORM_EPS) * g_ref[...]
    h_scr[...] = (y * (1.0 + sc_ref[0]) + sh_ref[0]).astype(bf16)
    off = 0
    for ref, width in zip(outs, SEG_W):
        for c0 in range(0, width, 512):
            cw = min(512, width - c0)
            ref[:, c0:c0 + cw] = jnp.dot(h_scr[...], w_ref[0, :, off + c0:off + c0 + cw],
                                         preferred_element_type=f32).astype(ref.dtype)
        off += width


def _inproj(x2d, shift, scale, g, w, layer, tm, rows_per_mod):
    m, d = x2d.shape
    tpm = rows_per_mod // tm
    mod_spec = pl.BlockSpec((1, 1, d), lambda i: (i // tpm, 0, 0))
    return pl.pallas_call(
        _inproj_kernel,
        grid=(m // tm,),
        in_specs=[pl.BlockSpec((tm, d), lambda i: (i, 0)), mod_spec, mod_spec,
                  pl.BlockSpec((1, d), lambda i: (0, 0)),
                  pl.BlockSpec((1, d, IN_W_PAD), lambda i: (layer, 0, 0), pipeline_mode=pl.Buffered(1))],
        out_specs=[pl.BlockSpec((tm, wd), lambda i: (i, 0)) for wd in SEG_W],
        out_shape=[jax.ShapeDtypeStruct((m, wd), dt) for wd, dt in zip(SEG_W, SEG_DT)],
        scratch_shapes=[pltpu.VMEM((tm, d), bf16)],
        compiler_params=_cparams(("arbitrary",)),
        name="inproj",
    )(x2d, shift, scale, g, w)


def _norm_rope(v, g, cos, sins):
    y = v * lax.rsqrt(jnp.mean(v * v, axis=-1, keepdims=True) + NORM_EPS) * g
    if cos is None:
        return y
    lane = lax.broadcasted_iota(jnp.int32, y.shape, 1)
    quarter = HEAD_DIM // 4
    partner = jnp.where((lane % (2 * quarter)) < quarter,
                        pltpu.roll(y, HEAD_DIM - quarter, 1), pltpu.roll(y, quarter, 1))
    return y * cos + partner * sins


def _kprep_kernel(k_ref, g_ref, cos_ref, sin_ref, o_ref, *, rope):
    for hh in range(ATT_KV_HEADS):
        sl = slice(hh * HEAD_DIM, (hh + 1) * HEAD_DIM)
        v = k_ref[:, sl].astype(f32)
        o_ref[:, sl] = _norm_rope(v, g_ref[...], cos_ref[...] if rope else None,
                                  sin_ref[...] if rope else None).astype(bf16)


def _kprep(k2d, g, cos, sins, seq, rope):
    m = k2d.shape[0]
    tk = min(512, seq)
    nt = seq // tk
    tab = pl.BlockSpec((tk, HEAD_DIM), lambda i: (i % nt, 0))
    return pl.pallas_call(
        functools.partial(_kprep_kernel, rope=rope),
        grid=(m // tk,),
        in_specs=[pl.BlockSpec((tk, ATT_KV_W), lambda i: (i, 0)),
                  pl.BlockSpec((1, HEAD_DIM), lambda i: (0, 0)), tab, tab],
        out_specs=pl.BlockSpec((tk, ATT_KV_W), lambda i: (i, 0)),
        out_shape=jax.ShapeDtypeStruct((m, ATT_KV_W), bf16),
        compiler_params=_cparams(("arbitrary",)),
        name="kprep_rope" if rope else "kprep",
    )(k2d, g, cos, sins)


def _attn_kernel(*refs, band, tq, seq):
    if band:
        (sink_ref, q_ref, cos_ref, sin_ref, qg_ref, kp_ref, km_ref, kn_ref, vp_ref, vm_ref, vn_ref,
         kc_ref, vc_ref, o_ref) = refs
    else:
        sink_ref, q_ref, qg_ref, kc_ref, vc_ref, o_ref = refs
    h = pl.program_id(1)
    n = pl.program_id(2)
    qscale = HEAD_DIM ** -0.5 * LOG2E
    kc = kc_ref[0]
    vc = vc_ref[0]
    rows = ATT_GROUP * ATT_BLOCK
    if band:
        kwin = jnp.concatenate([kp_ref[0], km_ref[0], kn_ref[0]], axis=0)
        vwin = jnp.concatenate([vp_ref[0], vm_ref[0], vn_ref[0]], axis=0)
        ri = lax.broadcasted_iota(jnp.int32, (rows, 3 * ATT_BLOCK), 0) % ATT_BLOCK
        ci = lax.broadcasted_iota(jnp.int32, (rows, 3 * ATT_BLOCK), 1)
        band_bias = jnp.where(ci >= ri, jnp.where(ci <= ri + 2 * ATT_WINDOW, 0.0, NEG_INF), NEG_INF)
        col = lax.broadcasted_iota(jnp.int32, (1, 3 * ATT_BLOCK), 1)
    sinkcol = jnp.concatenate(
        [jnp.full((ATT_BLOCK, 1), sink_ref[h * ATT_GROUP + hh] * LOG2E, f32) for hh in range(ATT_GROUP)], axis=0)
    nt = (((1,), (1,)), ((), ()))
    for jb in range(tq // ATT_BLOCK):
        r0 = jb * ATT_BLOCK
        qparts = []
        for hh in range(ATT_GROUP):
            qv = q_ref[0, r0:r0 + ATT_BLOCK, hh * HEAD_DIM:(hh + 1) * HEAD_DIM].astype(f32)
            if band:
                qv = _norm_rope(qv, qg_ref[...], cos_ref[r0:r0 + ATT_BLOCK, :], sin_ref[r0:r0 + ATT_BLOCK, :])
            else:
                qv = _norm_rope(qv, qg_ref[...], None, None)
            qparts.append((qv * qscale).astype(bf16))
        qs = jnp.concatenate(qparts, axis=0)
        s_c = lax.dot_general(qs, kc, nt, preferred_element_type=f32)
        m = jnp.maximum(jnp.max(s_c, axis=-1, keepdims=True), sinkcol)
        if band:
            kb = kwin[r0:r0 + 3 * ATT_BLOCK]
            vb = vwin[r0:r0 + 3 * ATT_BLOCK]
            kpos = col + (n * tq + r0 - ATT_BLOCK)
            col_bias = jnp.where(kpos >= 0, jnp.where(kpos < seq, 0.0, NEG_INF), NEG_INF)
            s_w = lax.dot_general(qs, kb, nt, preferred_element_type=f32) + (band_bias + col_bias)
            m = jnp.maximum(m, jnp.max(s_w, axis=-1, keepdims=True))
            p_w = jnp.exp2(s_w - m)
        p_c = jnp.exp2(s_c - m)
        den = jnp.sum(p_c, axis=-1, keepdims=True) + jnp.exp2(sinkcol - m)
        o = jnp.dot(p_c.astype(bf16), vc, preferred_element_type=f32)
        if band:
            den = den + jnp.sum(p_w, axis=-1, keepdims=True)
            o = o + jnp.dot(p_w.astype(bf16), vb, preferred_element_type=f32)
        o = o / den
        for hh in range(ATT_GROUP):
            o_ref[0, r0:r0 + ATT_BLOCK, hh * HEAD_DIM:(hh + 1) * HEAD_DIM] = (
                o[hh * ATT_BLOCK:(hh + 1) * ATT_BLOCK].astype(bf16))


def _attention(q, qg, sink, kc, vc, k=None, v=None, cos=None, sins=None):
    b, t, _ = q.shape
    lc = kc.shape[1]
    band = k is not None
    tq = ATT_TQ if t % ATT_TQ == 0 else 2 * ATT_BLOCK
    gw = ATT_GROUP * HEAD_DIM
    nblk = t // ATT_BLOCK
    per = tq // ATT_BLOCK
    smem = pl.BlockSpec(memory_space=pltpu.SMEM)
    qspec = pl.BlockSpec((1, tq, gw), lambda bi, h, n: (bi, n, h))
    gspec = pl.BlockSpec((1, HEAD_DIM), lambda bi, h, n: (0, 0))
    cspec = pl.BlockSpec((1, lc, HEAD_DIM), lambda bi, h, n: (bi, 0, h))
    if band:
        tab = pl.BlockSpec((tq, HEAD_DIM), lambda bi, h, n: (n, 0))
        prev = pl.BlockSpec((1, ATT_BLOCK, HEAD_DIM), lambda bi, h, n: (bi, jnp.maximum(n * per - 1, 0), h))
        main = pl.BlockSpec((1, tq, HEAD_DIM), lambda bi, h, n: (bi, n, h))
        nxt = pl.BlockSpec((1, ATT_BLOCK, HEAD_DIM), lambda bi, h, n: (bi, jnp.minimum((n + 1) * per, nblk - 1), h))
        in_specs = [smem, qspec, tab, tab, gspec, prev, main, nxt, prev, main, nxt, cspec, cspec]
        args = (sink, q, cos, sins, qg, k, k, k, v, v, v, kc, vc)
    else:
        in_specs = [smem, qspec, gspec, cspec, cspec]
        args = (sink, q, qg, kc, vc)
    return pl.pallas_call(
        functools.partial(_attn_kernel, band=band, tq=tq, seq=t),
        grid=(b, ATT_KV_HEADS, t // tq),
        in_specs=in_specs,
        out_specs=pl.BlockSpec((1, tq, gw), lambda bi, h, n: (bi, n, h)),
        out_shape=jax.ShapeDtypeStruct((b, t, ATT_W), bf16),
        compiler_params=_cparams(("arbitrary", "arbitrary", "arbitrary")),
        name="attn_band" if band else "attn_ctx",
    )(*args)


def _ssd_kernel(xc_ref, bc_ref, cc_ref, dtc_ref, zc_ref, xl_ref, bl_ref, cl_ref, dtl_ref, zl_ref,
                cw_ref, cb_ref, dtb_ref, a_ref, acol_ref, dsk_ref, ng_ref, oc_ref, ol_ref,
                pad_scr, xs_scr, cm_scr, bt_scr, dts_scr, dtt_scr, y_scr, ydir_scr, h_scr, *, lc, seq):
    ck = SSD_CHUNK
    gw = SSD_GW
    nst = SSD_STATE
    row = lax.broadcasted_iota(jnp.int32, (ck, ck), 0)
    col = lax.broadcasted_iota(jnp.int32, (ck, ck), 1)
    tri = (row >= col, col >= row)
    tri_bf = tuple(jnp.where(t, 1.0, 0.0).astype(bf16) for t in tri)
    first = col < SSD_HEAD_DIM
    h_scr[...] = jnp.zeros(h_scr.shape, f32)

    def run_seq(t, x_ref, b_ref, c_ref, dt_ref, z_ref, o_ref):
        nc = t // ck
        zero8 = jnp.zeros((8, gw + 2 * nst), f32)
        pad_scr[0:8, :] = zero8
        pad_scr[8 + t:16 + t, :] = zero8

        def fill(c, carry):
            r0 = pl.multiple_of(c * ck, ck)
            pad_scr[pl.ds(r0 + 8, ck), 0:gw] = x_ref[0, pl.ds(r0, ck), :].astype(f32)
            pad_scr[pl.ds(r0 + 8, ck), gw:gw + nst] = b_ref[0, pl.ds(r0, ck), :].astype(f32)
            pad_scr[pl.ds(r0 + 8, ck), gw + nst:gw + 2 * nst] = c_ref[0, pl.ds(r0, ck), :].astype(f32)
            return carry

        lax.fori_loop(0, nc, fill, 0)

        def conv(c, carry):
            r0 = pl.multiple_of(c * ck, ck)
            xm, x0, xp = _shifted_rows(pad_scr, r0, ck)
            act = _silu(cw_ref[0, 0:1, :] * xm + cw_ref[0, 1:2, :] * x0 + cw_ref[0, 2:3, :] * xp + cb_ref[0])
            xs = act[:, 0:gw]
            y_scr[pl.ds(r0, ck), :] = dsk_ref[...] * xs
            xs_scr[pl.ds(r0, ck), :] = xs.astype(bf16)
            bt_scr[c] = act[:, gw:gw + nst].T
            cm_scr[pl.ds(r0, ck), :] = act[:, gw + nst:gw + 2 * nst].astype(bf16)
            dv = dt_ref[0, pl.ds(r0, ck), :] + dtb_ref[0]
            dts = jnp.maximum(dv, 0.0) + jnp.log1p(jnp.exp(-jnp.abs(dv)))
            dts_scr[pl.ds(r0, ck), :] = dts
            dtt_scr[c] = dts.T[0:DT_ROWS, :]
            return carry

        lax.fori_loop(0, nc, conv, 0)

        def chunk(i, carry):
            for d in range(2):
                c = i if d == 0 else nc - 1 - i
                r0 = pl.multiple_of(c * ck, ck)
                cb16 = cm_scr[pl.ds(r0, ck), :]
                bt32 = bt_scr[c]
                dt = dts_scr[pl.ds(r0, ck), :]
                dt_t = dtt_scr[c]
                hi, mid, lo = _split3(dt * a_ref[0])
                cum = (jnp.dot(tri_bf[d], hi, preferred_element_type=f32)
                       + jnp.dot(tri_bf[d], mid, preferred_element_type=f32)
                       + jnp.dot(tri_bf[d], lo, preferred_element_type=f32))
                hi_t, mid_t, lo_t = _split3(dt_t * acol_ref[0, 0])
                cum_t = (jnp.dot(hi_t, tri_bf[1 - d], preferred_element_type=f32)
                         + jnp.dot(mid_t, tri_bf[1 - d], preferred_element_type=f32)
                         + jnp.dot(lo_t, tri_bf[1 - d], preferred_element_type=f32))
                tot = cum[ck - 1:ck, :] if d == 0 else cum[0:1, :]
                tot_t = cum_t[:, ck - 1:ck] if d == 0 else cum_t[:, 0:1]
                w_t = dt_t * jnp.exp(tot_t - cum_t)
                etot = jnp.exp(tot)
                cbm = jnp.dot(cb16, bt32.astype(bf16), preferred_element_type=f32)
                for p in range(SSD_KH // 2):
                    mks, ecols, sts = [], [], []
                    xp16 = xs_scr[pl.ds(r0, ck), p * LANES:(p + 1) * LANES]
                    for k in (2 * p, 2 * p + 1):
                        j = d * SSD_KH + k
                        ccol = jnp.broadcast_to(cum[:, j:j + 1], (ck, ck))
                        decay = jnp.exp(jnp.where(tri[d], ccol - cum_t[j:j + 1, :], NEG_INF))
                        mks.append((cbm * decay * dt_t[j:j + 1, :]).astype(bf16))
                        ecols.append(jnp.exp(ccol))
                        sts.append(jnp.dot((bt32 * w_t[j:j + 1, :]).astype(bf16), xp16, preferred_element_type=f32))
                    j0 = d * SSD_KH + 2 * p
                    hp = h_scr[d * (SSD_KH // 2) + p]
                    ydiag = jnp.where(first, jnp.dot(mks[0], xp16, preferred_element_type=f32),
                                      jnp.dot(mks[1], xp16, preferred_element_type=f32))
                    yoff = jnp.where(first, ecols[0], ecols[1]) * jnp.dot(cb16, hp.astype(bf16),
                                                                          preferred_element_type=f32)
                    ydir_scr[d, pl.ds(r0, ck), p * LANES:(p + 1) * LANES] = ydiag + yoff
                    erow = jnp.where(first[0:1, :], etot[:, j0:j0 + 1], etot[:, j0 + 1:j0 + 2])
                    h_scr[d * (SSD_KH // 2) + p] = hp * erow + jnp.where(first, sts[0], sts[1])
            return carry

        lax.fori_loop(0, nc, chunk, 0)

        def gate(c, carry):
            r0 = pl.multiple_of(c * ck, ck)
            y = y_scr[pl.ds(r0, ck), :] + ydir_scr[0, pl.ds(r0, ck), :] + ydir_scr[1, pl.ds(r0, ck), :]
            u = y * _silu(z_ref[0, pl.ds(r0, ck), :].astype(f32))
            u = u * lax.rsqrt(jnp.mean(u * u, axis=-1, keepdims=True) + NORM_EPS)
            o_ref[0, pl.ds(r0, ck), :] = (u * ng_ref[...]).astype(bf16)
            return carry

        lax.fori_loop(0, nc, gate, 0)

    run_seq(lc, xc_ref, bc_ref, cc_ref, dtc_ref, zc_ref, oc_ref)
    run_seq(seq, xl_ref, bl_ref, cl_ref, dtl_ref, zl_ref, ol_ref)


def _ssd(xbc_c, dt_c, z_c, xbc_l, dt_l, z_l, cw, cb, dtb, a_neg, a_col, dsk, ng):
    b, lc, _ = xbc_c.shape
    seq = xbc_l.shape[1]
    gw, nst = SSD_GW, SSD_STATE
    xoff = SSD_W // nst

    def seq_specs(t):
        return [pl.BlockSpec((1, t, gw), lambda bi, g: (bi, 0, g)),
                pl.BlockSpec((1, t, nst), lambda bi, g: (bi, 0, xoff + g)),
                pl.BlockSpec((1, t, nst), lambda bi, g: (bi, 0, xoff + SSD_GROUPS + g)),
                pl.BlockSpec((1, t, LANES), lambda bi, g: (bi, 0, g)),
                pl.BlockSpec((1, t, gw), lambda bi, g: (bi, 0, g))]

    cwid = gw + 2 * nst
    par_specs = [pl.BlockSpec((1, 3, cwid), lambda bi, g: (g, 0, 0)),
                 pl.BlockSpec((1, 1, cwid), lambda bi, g: (g, 0, 0)),
                 pl.BlockSpec((1, 1, LANES), lambda bi, g: (g, 0, 0)),
                 pl.BlockSpec((1, 1, LANES), lambda bi, g: (g, 0, 0)),
                 pl.BlockSpec((1, 1, DT_ROWS, 1), lambda bi, g: (g, 0, 0, 0)),
                 pl.BlockSpec((1, gw), lambda bi, g: (0, g)),
                 pl.BlockSpec((1, gw), lambda bi, g: (0, g))]
    nck = seq // SSD_CHUNK
    return pl.pallas_call(
        functools.partial(_ssd_kernel, lc=lc, seq=seq),
        grid=(b, SSD_GROUPS),
        in_specs=seq_specs(lc) + seq_specs(seq) + par_specs,
        out_specs=[pl.BlockSpec((1, lc, gw), lambda bi, g: (bi, 0, g)),
                   pl.BlockSpec((1, seq, gw), lambda bi, g: (bi, 0, g))],
        out_shape=[jax.ShapeDtypeStruct((b, lc, SSD_W), bf16), jax.ShapeDtypeStruct((b, seq, SSD_W), bf16)],
        scratch_shapes=[pltpu.VMEM((seq + 16, cwid), f32),
                        pltpu.VMEM((seq, gw), bf16),
                        pltpu.VMEM((seq, nst), bf16),
                        pltpu.VMEM((nck, nst, SSD_CHUNK), f32),
                        pltpu.VMEM((seq, LANES), f32),
                        pltpu.VMEM((nck, DT_ROWS, SSD_CHUNK), f32),
                        pltpu.VMEM((seq, gw), f32),
                        pltpu.VMEM((2, seq, gw), f32),
                        pltpu.VMEM((SSD_KH, nst, LANES), f32)],
        compiler_params=_cparams(("arbitrary", "arbitrary")),
        name="ssd",
    )(xbc_c, xbc_c, xbc_c, dt_c, z_c, xbc_l, xbc_l, xbc_l, dt_l, z_l, cw, cb, dtb, a_neg, a_col, dsk, ng)


def _sconv_kernel(b_ref, c_ref, h_ref, w_ref, o_ref, pad_scr, *, t):
    ck = min(256, t)
    zero8 = jnp.zeros((8, LANES), f32)
    pad_scr[0:8, :] = zero8
    pad_scr[8 + t:16 + t, :] = zero8

    def fill(c, carry):
        r0 = pl.multiple_of(c * ck, ck)
        pad_scr[pl.ds(r0 + 8, ck), :] = c_ref[0, pl.ds(r0, ck), :].astype(f32) * h_ref[0, pl.ds(r0, ck), :].astype(f32)
        return carry

    lax.fori_loop(0, t // ck, fill, 0)

    def conv(c, carry):
        r0 = pl.multiple_of(c * ck, ck)
        xm, x0, xp = _shifted_rows(pad_scr, r0, ck)
        acc = w_ref[0:1, :] * xm + w_ref[1:2, :] * x0 + w_ref[2:3, :] * xp
        o_ref[0, pl.ds(r0, ck), :] = (b_ref[0, pl.ds(r0, ck), :].astype(f32) * acc).astype(bf16)
        return carry

    lax.fori_loop(0, t // ck, conv, 0)


def _sconv(scb, scc, sch, w):
    b, t, cw = scb.shape
    spec = pl.BlockSpec((1, t, LANES), lambda bi, j: (bi, 0, j))
    return pl.pallas_call(
        functools.partial(_sconv_kernel, t=t),
        grid=(b, cw // LANES),
        in_specs=[spec, spec, spec, pl.BlockSpec((3, LANES), lambda bi, j: (0, j))],
        out_specs=spec,
        out_shape=jax.ShapeDtypeStruct((b, t, cw), bf16),
        scratch_shapes=[pltpu.VMEM((t + 16, LANES), f32)],
        compiler_params=_cparams(("arbitrary", "arbitrary")),
        name="sconv",
    )(scb, scc, sch, w)


def _outproj_kernel(att_ref, ssd_ref, sc_ref, x_ref, g1_ref, sh2_ref, sc2_ref, n2_ref, wo_ref, wr_ref,
                    xo_ref, h2_ref, lg_ref, *, sub):
    for r0 in range(0, x_ref.shape[0], sub):
        rs = slice(r0, r0 + sub)
        acc = jnp.dot(att_ref[rs, :], wo_ref[0, 0:ATT_W, :], preferred_element_type=f32)
        acc = acc + jnp.dot(ssd_ref[rs, :], wo_ref[0, ATT_W:ATT_W + SSD_W, :], preferred_element_type=f32)
        acc = acc + jnp.dot(sc_ref[rs, :], wo_ref[0, ATT_W + SSD_W:, :], preferred_element_type=f32)
        x = x_ref[rs, :] + g1_ref[0] * acc
        xo_ref[rs, :] = x
        y = x * lax.rsqrt(jnp.mean(x * x, axis=-1, keepdims=True) + NORM_EPS) * n2_ref[...]
        h2 = y * (1.0 + sc2_ref[0]) + sh2_ref[0]
        hh = h2.astype(bf16)
        h2_ref[rs, :] = hh
        hm = (h2 - hh.astype(f32)).astype(bf16)
        r1 = jnp.dot(hh, wr_ref[0], preferred_element_type=f32)
        r2 = jnp.dot(hm, wr_ref[0, :, 0:ROUTER_PAD], preferred_element_type=f32)
        lg_ref[rs, :] = r1[:, 0:ROUTER_PAD] + r1[:, ROUTER_PAD:] + r2


def _outproj(att, ssd, sconv, x2d, g1, sh2, sc2, n2, wo, wr2, layer, tm, rows_per_mod):
    m, d = x2d.shape
    tpm = rows_per_mod // tm
    mod_spec = pl.BlockSpec((1, 1, d), lambda i: (i // tpm, 0, 0))
    return pl.pallas_call(
        functools.partial(_outproj_kernel, sub=min(256, tm)),
        grid=(m // tm,),
        in_specs=[pl.BlockSpec((tm, ATT_W), lambda i: (i, 0)), pl.BlockSpec((tm, SSD_W), lambda i: (i, 0)),
                  pl.BlockSpec((tm, SC_W), lambda i: (i, 0)), pl.BlockSpec((tm, d), lambda i: (i, 0)),
                  mod_spec, mod_spec, mod_spec, pl.BlockSpec((1, d), lambda i: (0, 0)),
                  pl.BlockSpec((1, d, d), lambda i: (layer, 0, 0)),
                  pl.BlockSpec((1, d, 2 * ROUTER_PAD), lambda i: (layer, 0, 0))],
        out_specs=[pl.BlockSpec((tm, d), lambda i: (i, 0)), pl.BlockSpec((tm, d), lambda i: (i, 0)),
                   pl.BlockSpec((tm, ROUTER_PAD), lambda i: (i, 0))],
        out_shape=[jax.ShapeDtypeStruct((m, d), f32), jax.ShapeDtypeStruct((m, d), bf16),
                   jax.ShapeDtypeStruct((m, ROUTER_PAD), f32)],
        compiler_params=_cparams(("arbitrary",)),
        name="outproj",
    )(att, ssd, sconv, x2d, g1, sh2, sc2, n2, wo, wr2)


def _ffn_kernel(x_ref, gv_ref, wg_ref, wu_ref, wd_ref, o_ref):
    x = x_ref[0]
    a = jnp.dot(x, wg_ref[0, 0], preferred_element_type=f32)
    u = jnp.dot(x, wu_ref[0, 0], preferred_element_type=f32)
    hmid = (_silu(a) * u).astype(bf16)
    o_ref[0] = (jnp.dot(hmid, wd_ref[0, 0], preferred_element_type=f32) * gv_ref[0]).astype(bf16)


def _ffn_tile(nrows):
    for nt in range(1, nrows + 1):
        if nrows % nt == 0 and (nrows // nt) % 16 == 0 and nrows // nt <= FFN_MAX_TILE:
            return nrows // nt
    raise ValueError(f"no aligned row tile for {nrows} gathered rows")


def _expert_ffn(xg, gv, wg, wu, wd, layer):
    e, r, d = xg.shape
    ff = wg.shape[3]
    tm = _ffn_tile(r)
    return pl.pallas_call(
        _ffn_kernel,
        grid=(e, r // tm),
        in_specs=[pl.BlockSpec((1, tm, d), lambda ei, i: (ei, i, 0)),
                  pl.BlockSpec((1, tm, 1), lambda ei, i: (ei, i, 0)),
                  pl.BlockSpec((1, 1, d, ff), lambda ei, i: (layer, ei, 0, 0)),
                  pl.BlockSpec((1, 1, d, ff), lambda ei, i: (layer, ei, 0, 0)),
                  pl.BlockSpec((1, 1, ff, d), lambda ei, i: (layer, ei, 0, 0))],
        out_specs=pl.BlockSpec((1, tm, d), lambda ei, i: (ei, i, 0)),
        out_shape=jax.ShapeDtypeStruct((e, r, d), bf16),
        compiler_params=_cparams(("arbitrary", "arbitrary")),
        name="expert_ffn",
    )(xg, gv, wg, wu, wd)


def _combine_kernel(hit_ref, idx_ref, y_ref, xm_ref, g2_ref, o_ref, acc_scr, *, sb):
    bi = pl.program_id(0)
    k = pl.program_id(1)
    e = pl.program_id(2)
    nb, nk, ne = pl.num_programs(0), pl.num_programs(1), pl.num_programs(2)
    tt = acc_scr.shape[0]
    nsb = y_ref.shape[1] // sb

    @pl.when(e == 0)
    def _():
        acc_scr[...] = jnp.zeros(acc_scr.shape, f32)

    tok = lax.broadcasted_iota(jnp.int32, (tt, sb), 0) + k * tt
    for j in range(nsb):
        @pl.when(hit_ref[((e * nb + bi) * nk + k) * nsb + j] != 0)
        def _():
            onehot = jnp.where(tok == idx_ref[0, 0, :, j * sb:(j + 1) * sb], 1.0, 0.0).astype(bf16)
            acc_scr[...] += jnp.dot(onehot, y_ref[0, j * sb:(j + 1) * sb, :], preferred_element_type=f32)

    @pl.when(e == ne - 1)
    def _():
        o_ref[...] = xm_ref[...] + g2_ref[0] * acc_scr[...]


def _combine(idx, y, xm, g2, b, t, per_sample_gate):
    e, _, _, cap = idx.shape
    d = xm.shape[1]
    tt = min(COMBINE_TILE, t)
    nk = t // tt
    sb = min(COMBINE_SLOTS, cap)
    nsb = cap // sb
    blk = idx.reshape(e, b, nsb, sb)
    lo = blk[..., 0][:, :, None, :]
    hi = blk[..., sb - 1][:, :, None, :]
    tile0 = (jnp.arange(nk, dtype=idx.dtype) * tt)[None, None, :, None]
    hit = ((lo < tile0 + tt) & (hi >= tile0)).astype(jnp.int32).reshape(-1)
    grid_spec = pltpu.PrefetchScalarGridSpec(
        num_scalar_prefetch=1,
        grid=(b, nk, e),
        in_specs=[pl.BlockSpec((1, 1, 1, cap), lambda bi, k, ei, hit_ref: (ei, bi, 0, 0)),
                  pl.BlockSpec((1, cap, d), lambda bi, k, ei, hit_ref: (ei, bi, 0)),
                  pl.BlockSpec((tt, d), lambda bi, k, ei, hit_ref: (bi * nk + k, 0)),
                  pl.BlockSpec((1, 1, d), lambda bi, k, ei, hit_ref: (bi if per_sample_gate else 0, 0, 0))],
        out_specs=pl.BlockSpec((tt, d), lambda bi, k, ei, hit_ref: (bi * nk + k, 0)),
        scratch_shapes=[pltpu.VMEM((tt, d), f32)])
    return pl.pallas_call(
        functools.partial(_combine_kernel, sb=sb),
        grid_spec=grid_spec,
        out_shape=jax.ShapeDtypeStruct(xm.shape, f32),
        compiler_params=_cparams(("arbitrary", "arbitrary", "arbitrary")),
        name="moe_combine",
    )(hit, idx, y, xm, g2)


def _rope_tables(n_tokens):
    rows = n_tokens // GRID_W
    row = jnp.repeat(jnp.arange(rows), GRID_W).astype(f32)
    col = jnp.tile(jnp.arange(GRID_W), rows).astype(f32)
    n_freq = HEAD_DIM // 4
    inv = ROPE_THETA ** (-jnp.arange(n_freq, dtype=f32) / n_freq)
    ang_r = row[:, None] * inv
    ang_c = col[:, None] * inv
    cos = jnp.concatenate([jnp.cos(ang_r), jnp.cos(ang_r), jnp.cos(ang_c), jnp.cos(ang_c)], axis=-1)
    sins = jnp.concatenate([-jnp.sin(ang_r), jnp.sin(ang_r), -jnp.sin(ang_c), jnp.sin(ang_c)], axis=-1)
    return cos, sins


def _dt_layout(v):
    lead = v.shape[:-1]
    v = v.reshape(lead + (2, SSD_GROUPS, SSD_KH))
    v = jnp.moveaxis(v, -2, -3).reshape(lead + (SSD_GROUPS, 2 * SSD_KH))
    v = jnp.pad(v, [(0, 0)] * (len(lead) + 1) + [(0, LANES - 2 * SSD_KH)])
    return v.reshape(lead + (DT_PAD,))


def _group_layout(v):
    outs = []
    for g in range(SSD_GROUPS):
        outs.append(jnp.concatenate([
            v[..., g * SSD_GW:(g + 1) * SSD_GW],
            v[..., SSD_W + g * SSD_STATE:SSD_W + (g + 1) * SSD_STATE],
            v[..., SSD_W + SSD_BC_W + g * SSD_STATE:SSD_W + SSD_BC_W + (g + 1) * SSD_STATE]], axis=-1))
    return jnp.stack(outs)


def _route(logits, b, t):
    cap = EC_FACTOR * t // N_EXPERTS
    aff = jax.nn.softmax(logits[:, :N_EXPERTS].reshape(b, t, N_EXPERTS), axis=-1)
    gval, idx = lax.top_k(jnp.swapaxes(aff, 1, 2), cap)
    idx, gval = lax.sort((idx, gval), dimension=-1, num_keys=1)
    rows = idx + (jnp.arange(b, dtype=idx.dtype) * t)[:, None, None]
    to_e = lambda u: jnp.swapaxes(u, 0, 1).reshape(N_EXPERTS, b * cap)
    return to_e(gval)[..., None], to_e(rows), jnp.swapaxes(idx, 0, 1).reshape(N_EXPERTS, b, 1, cap)


def kernel(x, c, ctx, c_ctx, norm1_g, norm2_g, w_mod, b_mod, w_in, q_norm_g, k_norm_g, attn_sink, ssd_conv_w,
           ssd_conv_b, ssd_dt_bias, ssd_a_log, ssd_d, ssd_norm_g, sc_conv_w, w_out, w_router, w_expert_gate,
           w_expert_up, w_expert_down):
    b, s, d = x.shape
    lc = ctx.shape[1]
    nl = w_in.shape[0]
    cos, sins = _rope_tables(s)

    nrow = -(-(b + 1) // 8) * 8
    cvec = jnp.zeros((nrow, d), f32).at[:b].set(c).at[b].set(c_ctx)
    mods = _modulations(cvec, w_mod, b_mod).reshape(nl, nrow, N_MOD, 1, d)

    cuts = [0]
    for wdt in (ATT_W, ATT_KV_W, ATT_KV_W, SSD_W, SSD_XBC_W, SSD_DT_W, SC_W, SC_W, SC_W):
        cuts.append(cuts[-1] + wdt)
    w_dt = _dt_layout(w_in[:, :, cuts[5]:cuts[6]])
    w_in_p = jnp.concatenate([w_in[:, :, :cuts[5]], w_dt, w_in[:, :, cuts[6]:]], axis=-1).astype(bf16)
    w_out_b = w_out.astype(bf16)
    wr_pad = jnp.pad(w_router, ((0, 0), (0, 0), (0, ROUTER_PAD - N_EXPERTS)))
    wr_hi, wr_mid, _ = _split3(wr_pad)
    wr2 = jnp.concatenate([wr_hi, wr_mid], axis=-1)
    wg_b, wu_b, wd_b = w_expert_gate.astype(bf16), w_expert_up.astype(bf16), w_expert_down.astype(bf16)
    dtb = _dt_layout(ssd_dt_bias.reshape(nl, SSD_DT_W)).reshape(nl, SSD_GROUPS, 1, LANES)
    a_neg = _dt_layout(-jnp.exp(ssd_a_log.reshape(nl, SSD_DT_W))).reshape(nl, SSD_GROUPS, 1, LANES)
    a_col = a_neg[..., :DT_ROWS].reshape(nl, SSD_GROUPS, 1, DT_ROWS, 1)
    dsk = jnp.repeat(ssd_d, SSD_HEAD_DIM, axis=-1).reshape(nl, 1, SSD_W)

    xl = x.reshape(b * s, d)
    xc = ctx.reshape(b * lc, d)
    for i in range(nl):
        with_ctx_out = i < nl - 1
        sh1, sc1, g1, sh2, sc2, g2 = (mods[i, :, j] for j in range(N_MOD))
        n1 = norm1_g[i].reshape(1, d)
        n2 = norm2_g[i].reshape(1, d)
        qg = q_norm_g[i].reshape(1, HEAD_DIM)
        kg = k_norm_g[i].reshape(1, HEAD_DIM)

        q, k, v, z, xbc, dtr, scb, scc, sch = _inproj(xl, sh1, sc1, n1, w_in_p, i, 512, s)
        qc, kc, vc, zc, xbcc, dtrc, scbc, sccc, schc = _inproj(xc, sh1[b:], sc1[b:], n1, w_in_p, i, lc, b * lc)

        kp = _kprep(k, kg, cos, sins, s, True)
        kcp = _kprep(kc, kg, cos, sins, lc, False)
        r3 = lambda u, t: u.reshape(b, t, u.shape[-1])
        att = _attention(r3(q, s), qg, attn_sink[i], r3(kcp, lc), r3(vc, lc), r3(kp, s), r3(v, s), cos, sins)

        ssd_c, ssd_l = _ssd(r3(xbcc, lc), r3(dtrc, lc), r3(zc, lc), r3(xbc, s), r3(dtr, s), r3(z, s),
                            _group_layout(ssd_conv_w[i]), _group_layout(ssd_conv_b[i].reshape(1, -1)),
                            dtb[i], a_neg[i], a_col[i], dsk[i], ssd_norm_g[i].reshape(1, SSD_W))
        sconv = _sconv(r3(scb, s), r3(scc, s), r3(sch, s), sc_conv_w[i])

        xm, h2, lg = _outproj(att.reshape(b * s, ATT_W), ssd_l.reshape(b * s, SSD_W), sconv.reshape(b * s, SC_W),
                              xl, g1, sh2, sc2, n2, w_out_b, wr2, i, 512, s)
        gv, rows, idx = _route(lg, b, s)
        y = _expert_ffn(h2[rows], gv, wg_b, wu_b, wd_b, i)
        xl = _combine(idx, y, xm, g2, b, s, True)
        if with_ctx_out:
            attc = _attention(r3(qc, lc), qg, attn_sink[i], r3(kcp, lc), r3(vc, lc))
            sconvc = _sconv(r3(scbc, lc), r3(sccc, lc), r3(schc, lc), sc_conv_w[i])
            xmc, h2c, lgc = _outproj(attc.reshape(b * lc, ATT_W), ssd_c.reshape(b * lc, SSD_W),
                                     sconvc.reshape(b * lc, SC_W), xc, g1[b:], sh2[b:], sc2[b:], n2,
                                     w_out_b, wr2, i, lc, b * lc)
            gvc, rowsc, idxc = _route(lgc, b, lc)
            yc = _expert_ffn(h2c[rowsc], gvc, wg_b, wu_b, wd_b, i)
            xc = _combine(idxc, yc, xmc, g2[b:], b, lc, False)
    return xl.reshape(b, s, d)
```

```python
import functools

import jax
import jax.numpy as jnp
from jax import lax
from jax.experimental import pallas as pl
from jax.experimental.pallas import tpu as pltpu

f32 = jnp.float32
bf16 = jnp.bfloat16

D_MODEL = 2048
DEPTH = 4
GRID_W = 64
NORM_EPS = 1e-6
N_MOD = 6
HEAD_DIM = 128
ATT_HEADS = 8
ATT_KV_HEADS = 2
ATT_GROUP = ATT_HEADS // ATT_KV_HEADS
ATT_WINDOW = 128
ATT_BLOCK = 128
ROPE_THETA = 10000.0
SSD_HEAD_DIM = 64
SSD_W = 512
SSD_HEADS = 8
SSD_GROUPS = 2
SSD_KH = SSD_HEADS // SSD_GROUPS
SSD_STATE = 128
SSD_CHUNK = 128
SSD_GW = SSD_W // SSD_GROUPS
SC_W = 512
ATT_W = ATT_HEADS * HEAD_DIM
ATT_KV_W = ATT_KV_HEADS * HEAD_DIM
SSD_BC_W = SSD_GROUPS * SSD_STATE
SSD_XBC_W = SSD_W + 2 * SSD_BC_W
SSD_DT_W = 2 * SSD_HEADS
N_EXPERTS = 16
EC_FACTOR = 2
EXPERT_FF = 1024

LANES = 128
DT_PAD = SSD_GROUPS * LANES
ROUTER_PAD = LANES
SEG_W = (ATT_W, ATT_KV_W, ATT_KV_W, SSD_W, SSD_XBC_W, DT_PAD, SC_W, SC_W, SC_W)
SEG_DT = (bf16, bf16, bf16, bf16, bf16, f32, bf16, bf16, bf16)
IN_W_PAD = sum(SEG_W)
VMEM_LIMIT = 56 * 1024 * 1024
NEG_INF = float("-inf")
LOG2E = 1.4426950408889634
ATT_TQ = 512
COMBINE_TILE = 1024
COMBINE_SLOTS = 256
DT_ROWS = 16
FFN_MAX_TILE = 576


def _cparams(sem):
    return pltpu.CompilerParams(dimension_semantics=sem, vmem_limit_bytes=VMEM_LIMIT)


def _silu(v):
    return v * jax.nn.sigmoid(v)


def _shifted_rows(pad_ref, r0, n):
    x0 = pad_ref[pl.ds(r0 + 8, n), :]
    before = pad_ref[pl.ds(r0, 8), :][7:8, :]
    after = pad_ref[pl.ds(r0 + 8 + n, 8), :][0:1, :]
    ri = lax.broadcasted_iota(jnp.int32, x0.shape, 0)
    xm = jnp.where(ri == 0, before, pltpu.roll(x0, 1, 0))
    xp = jnp.where(ri == n - 1, after, pltpu.roll(x0, n - 1, 0))
    return xm, x0, xp


def _split3(v):
    hi = v.astype(bf16)
    r = v - hi.astype(f32)
    mid = r.astype(bf16)
    lo = (r - mid.astype(f32)).astype(bf16)
    return hi, mid, lo


def _mod_kernel(c_ref, w_ref, b_ref, o_ref):
    a = _silu(c_ref[...]).astype(bf16)
    o_ref[0] = jnp.dot(a, w_ref[0].astype(bf16), preferred_element_type=f32) + b_ref[0]


def _modulations(cvec, w_mod, b_mod):
    nl, d, n = w_mod.shape
    r = cvec.shape[0]
    tn = 1024
    return pl.pallas_call(
        _mod_kernel,
        grid=(nl, n // tn),
        in_specs=[pl.BlockSpec((r, d), lambda l, j: (0, 0)),
                  pl.BlockSpec((1, d, tn), lambda l, j: (l, 0, j)),
                  pl.BlockSpec((1, 1, tn), lambda l, j: (l, 0, j))],
        out_specs=pl.BlockSpec((1, r, tn), lambda l, j: (l, 0, j)),
        out_shape=jax.ShapeDtypeStruct((nl, r, n), f32),
        compiler_params=_cparams(("arbitrary", "arbitrary")),
        name="modulations",
    )(cvec, w_mod, b_mod.reshape(nl, 1, n))


def _inproj_kernel(x_ref, sh_ref, sc_ref, g_ref, w_ref, *rest):
    outs, h_scr = rest[:-1], rest[-1]
    x = x_ref[...]
    y = x * lax.rsqrt(jnp.mean(x * x, axis=-1, keepdims=True) + NORM_EPS) * g_ref[...]
    h_scr[...] = (y * (1.0 + sc_ref[0]) + sh_ref[0]).astype(bf16)
    off = 0
    for ref, width in zip(outs, SEG_W):
        for c0 in range(0, width, 512):
            cw = min(512, width - c0)
            ref[:, c0:c0 + cw] = jnp.dot(h_scr[...], w_ref[0, :, off + c0:off + c0 + cw],
                                         preferred_element_type=f32).astype(ref.dtype)
        off += width


def _inproj(x2d, shift, scale, g, w, layer, tm, rows_per_mod):
    m, d = x2d.shape
    tpm = rows_per_mod // tm
    mod_spec = pl.BlockSpec((1, 1, d), lambda i: (i // tpm, 0, 0))
    return pl.pallas_call(
        _inproj_kernel,
        grid=(m // tm,),
        in_specs=[pl.BlockSpec((tm, d), lambda i: (i, 0)), mod_spec, mod_spec,
                  pl.BlockSpec((1, d), lambda i: (0, 0)),
                  pl.BlockSpec((1, d, IN_W_PAD), lambda i: (layer, 0, 0), pipeline_mode=pl.Buffered(1))],
        out_specs=[pl.BlockSpec((tm, wd), lambda i: (i, 0)) for wd in SEG_W],
        out_shape=[jax.ShapeDtypeStruct((m, wd), dt) for wd, dt in zip(SEG_W, SEG_DT)],
        scratch_shapes=[pltpu.VMEM((tm, d), bf16)],
        compiler_params=_cparams(("arbitrary",)),
        name="inproj",
    )(x2d, shift, scale, g, w)


def _norm_rope(v, g, cos, sins):
    y = v * lax.rsqrt(jnp.mean(v * v, axis=-1, keepdims=True) + NORM_EPS) * g
    if cos is None:
        return y
    lane = lax.broadcasted_iota(jnp.int32, y.shape, 1)
    quarter = HEAD_DIM // 4
    partner = jnp.where((lane % (2 * quarter)) < quarter,
                        pltpu.roll(y, HEAD_DIM - quarter, 1), pltpu.roll(y, quarter, 1))
    return y * cos + partner * sins


def _kprep_kernel(k_ref, g_ref, cos_ref, sin_ref, o_ref, *, rope):
    for hh in range(ATT_KV_HEADS):
        sl = slice(hh * HEAD_DIM, (hh + 1) * HEAD_DIM)
        v = k_ref[:, sl].astype(f32)
        o_ref[:, sl] = _norm_rope(v, g_ref[...], cos_ref[...] if rope else None,
                                  sin_ref[...] if rope else None).astype(bf16)


def _kprep(k2d, g, cos, sins, seq, rope):
    m = k2d.shape[0]
    tk = min(512, seq)
    nt = seq // tk
    tab = pl.BlockSpec((tk, HEAD_DIM), lambda i: (i % nt, 0))
    return pl.pallas_call(
        functools.partial(_kprep_kernel, rope=rope),
        grid=(m // tk,),
        in_specs=[pl.BlockSpec((tk, ATT_KV_W), lambda i: (i, 0)),
                  pl.BlockSpec((1, HEAD_DIM), lambda i: (0, 0)), tab, tab],
        out_specs=pl.BlockSpec((tk, ATT_KV_W), lambda i: (i, 0)),
        out_shape=jax.ShapeDtypeStruct((m, ATT_KV_W), bf16),
        compiler_params=_cparams(("arbitrary",)),
        name="kprep_rope" if rope else "kprep",
    )(k2d, g, cos, sins)


def _attn_kernel(*refs, band, tq, seq):
    if band:
        (sink_ref, q_ref, cos_ref, sin_ref, qg_ref, kp_ref, km_ref, kn_ref, vp_ref, vm_ref, vn_ref,
         kc_ref, vc_ref, o_ref) = refs
    else:
        sink_ref, q_ref, qg_ref, kc_ref, vc_ref, o_ref = refs
    h = pl.program_id(1)
    n = pl.program_id(2)
    qscale = HEAD_DIM ** -0.5 * LOG2E
    kc = kc_ref[0]
    vc = vc_ref[0]
    rows = ATT_GROUP * ATT_BLOCK
    if band:
        kwin = jnp.concatenate([kp_ref[0], km_ref[0], kn_ref[0]], axis=0)
        vwin = jnp.concatenate([vp_ref[0], vm_ref[0], vn_ref[0]], axis=0)
        ri = lax.broadcasted_iota(jnp.int32, (rows, 3 * ATT_BLOCK), 0) % ATT_BLOCK
        ci = lax.broadcasted_iota(jnp.int32, (rows, 3 * ATT_BLOCK), 1)
        band_bias = jnp.where(ci >= ri, jnp.where(ci <= ri + 2 * ATT_WINDOW, 0.0, NEG_INF), NEG_INF)
        col = lax.broadcasted_iota(jnp.int32, (1, 3 * ATT_BLOCK), 1)
    sinkcol = jnp.concatenate(
        [jnp.full((ATT_BLOCK, 1), sink_ref[h * ATT_GROUP + hh] * LOG2E, f32) for hh in range(ATT_GROUP)], axis=0)
    nt = (((1,), (1,)), ((), ()))
    blocks = range(tq // ATT_BLOCK)
    qs, s_c, s_w, m, p_c, p_w, den, o = {}, {}, {}, {}, {}, {}, {}, {}
    for jb in blocks:
        r0 = jb * ATT_BLOCK
        qparts = []
        for hh in range(ATT_GROUP):
            qv = q_ref[0, r0:r0 + ATT_BLOCK, hh * HEAD_DIM:(hh + 1) * HEAD_DIM].astype(f32)
            if band:
                qv = _norm_rope(qv, qg_ref[...], cos_ref[r0:r0 + ATT_BLOCK, :], sin_ref[r0:r0 + ATT_BLOCK, :])
            else:
                qv = _norm_rope(qv, qg_ref[...], None, None)
            qparts.append((qv * qscale).astype(bf16))
        qs[jb] = jnp.concatenate(qparts, axis=0)
    for jb in blocks:
        r0 = jb * ATT_BLOCK
        s_c[jb] = lax.dot_general(qs[jb], kc, nt, preferred_element_type=f32)
        if band:
            kpos = col + (n * tq + r0 - ATT_BLOCK)
            col_bias = jnp.where(kpos >= 0, jnp.where(kpos < seq, 0.0, NEG_INF), NEG_INF)
            s_w[jb] = (lax.dot_general(qs[jb], kwin[r0:r0 + 3 * ATT_BLOCK], nt, preferred_element_type=f32)
                       + (band_bias + col_bias))
    def lane_tiles(*arrs):
        return [a[:, c0:c0 + LANES] for a in arrs for c0 in range(0, a.shape[1], LANES)]

    nct = kc.shape[0] // LANES
    stiles = {}
    for jb in blocks:
        stiles[jb] = lane_tiles(s_c[jb], s_w[jb]) if band else lane_tiles(s_c[jb])
        m[jb] = jnp.maximum(jnp.max(functools.reduce(jnp.maximum, stiles[jb]), axis=-1, keepdims=True), sinkcol)
    for jb in blocks:
        mb = jnp.broadcast_to(m[jb], (rows, LANES))
        ptiles = [jnp.exp2(t - mb) for t in stiles[jb]]
        den[jb] = jnp.sum(functools.reduce(jnp.add, ptiles), axis=-1, keepdims=True) + jnp.exp2(sinkcol - m[jb])
        p_c[jb] = jnp.concatenate([t.astype(bf16) for t in ptiles[:nct]], axis=1)
        if band:
            p_w[jb] = jnp.concatenate([t.astype(bf16) for t in ptiles[nct:]], axis=1)
    for jb in blocks:
        r0 = jb * ATT_BLOCK
        o[jb] = jnp.dot(p_c[jb], vc, preferred_element_type=f32)
        if band:
            o[jb] = o[jb] + jnp.dot(p_w[jb], vwin[r0:r0 + 3 * ATT_BLOCK], preferred_element_type=f32)
    for jb in blocks:
        r0 = jb * ATT_BLOCK
        res = o[jb] * jnp.broadcast_to(1.0 / den[jb], (rows, HEAD_DIM))
        for hh in range(ATT_GROUP):
            o_ref[0, r0:r0 + ATT_BLOCK, hh * HEAD_DIM:(hh + 1) * HEAD_DIM] = (
                res[hh * ATT_BLOCK:(hh + 1) * ATT_BLOCK].astype(bf16))


def _attention(q, qg, sink, kc, vc, k=None, v=None, cos=None, sins=None):
    b, t, _ = q.shape
    lc = kc.shape[1]
    band = k is not None
    tq = ATT_TQ if t % ATT_TQ == 0 else 2 * ATT_BLOCK
    gw = ATT_GROUP * HEAD_DIM
    nblk = t // ATT_BLOCK
    per = tq // ATT_BLOCK
    smem = pl.BlockSpec(memory_space=pltpu.SMEM)
    qspec = pl.BlockSpec((1, tq, gw), lambda bi, h, n: (bi, n, h))
    gspec = pl.BlockSpec((1, HEAD_DIM), lambda bi, h, n: (0, 0))
    cspec = pl.BlockSpec((1, lc, HEAD_DIM), lambda bi, h, n: (bi, 0, h))
    if band:
        tab = pl.BlockSpec((tq, HEAD_DIM), lambda bi, h, n: (n, 0))
        prev = pl.BlockSpec((1, ATT_BLOCK, HEAD_DIM), lambda bi, h, n: (bi, jnp.maximum(n * per - 1, 0), h))
        main = pl.BlockSpec((1, tq, HEAD_DIM), lambda bi, h, n: (bi, n, h))
        nxt = pl.BlockSpec((1, ATT_BLOCK, HEAD_DIM), lambda bi, h, n: (bi, jnp.minimum((n + 1) * per, nblk - 1), h))
        in_specs = [smem, qspec, tab, tab, gspec, prev, main, nxt, prev, main, nxt, cspec, cspec]
        args = (sink, q, cos, sins, qg, k, k, k, v, v, v, kc, vc)
    else:
        in_specs = [smem, qspec, gspec, cspec, cspec]
        args = (sink, q, qg, kc, vc)
    return pl.pallas_call(
        functools.partial(_attn_kernel, band=band, tq=tq, seq=t),
        grid=(b, ATT_KV_HEADS, t // tq),
        in_specs=in_specs,
        out_specs=pl.BlockSpec((1, tq, gw), lambda bi, h, n: (bi, n, h)),
        out_shape=jax.ShapeDtypeStruct((b, t, ATT_W), bf16),
        compiler_params=_cparams(("arbitrary", "arbitrary", "arbitrary")),
        name="attn_band" if band else "attn_ctx",
    )(*args)


def _ssd_kernel(xc_ref, bc_ref, cc_ref, dtc_ref, zc_ref, xl_ref, bl_ref, cl_ref, dtl_ref, zl_ref,
                cw_ref, cb_ref, dtb_ref, a_ref, acol_ref, dsk_ref, ng_ref, oc_ref, ol_ref,
                pad_scr, xs_scr, cm_scr, bt_scr, dts_scr, dtt_scr, y_scr, ydir_scr, h_scr, *, lc, seq):
    ck = SSD_CHUNK
    gw = SSD_GW
    nst = SSD_STATE
    row = lax.broadcasted_iota(jnp.int32, (ck, ck), 0)
    col = lax.broadcasted_iota(jnp.int32, (ck, ck), 1)
    tri = (row >= col, col >= row)
    tri_bf = tuple(jnp.where(t, 1.0, 0.0).astype(bf16) for t in tri)
    first = col < SSD_HEAD_DIM
    h_scr[...] = jnp.zeros(h_scr.shape, f32)

    def run_seq(t, x_ref, b_ref, c_ref, dt_ref, z_ref, o_ref):
        nc = t // ck
        zero8 = jnp.zeros((8, gw + 2 * nst), f32)
        pad_scr[0:8, :] = zero8
        pad_scr[8 + t:16 + t, :] = zero8

        def fill(c, carry):
            r0 = pl.multiple_of(c * ck, ck)
            pad_scr[pl.ds(r0 + 8, ck), 0:gw] = x_ref[0, pl.ds(r0, ck), :].astype(f32)
            pad_scr[pl.ds(r0 + 8, ck), gw:gw + nst] = b_ref[0, pl.ds(r0, ck), :].astype(f32)
            pad_scr[pl.ds(r0 + 8, ck), gw + nst:gw + 2 * nst] = c_ref[0, pl.ds(r0, ck), :].astype(f32)
            return carry

        lax.fori_loop(0, nc, fill, 0)

        def conv(c, carry):
            r0 = pl.multiple_of(c * ck, ck)
            xm, x0, xp = _shifted_rows(pad_scr, r0, ck)
            act = _silu(cw_ref[0, 0:1, :] * xm + cw_ref[0, 1:2, :] * x0 + cw_ref[0, 2:3, :] * xp + cb_ref[0])
            xs = act[:, 0:gw]
            y_scr[pl.ds(r0, ck), :] = dsk_ref[...] * xs
            xs_scr[pl.ds(r0, ck), :] = xs.astype(bf16)
            bt_scr[c] = act[:, gw:gw + nst].T
            cm_scr[pl.ds(r0, ck), :] = act[:, gw + nst:gw + 2 * nst].astype(bf16)
            dv = dt_ref[0, pl.ds(r0, ck), :] + dtb_ref[0]
            dts = jnp.maximum(dv, 0.0) + jnp.log1p(jnp.exp(-jnp.abs(dv)))
            dts_scr[pl.ds(r0, ck), :] = dts
            dtt_scr[c] = dts.T[0:DT_ROWS, :]
            return carry

        lax.fori_loop(0, nc, conv, 0)

        def chunk(i, carry):
            dirs = (0, 1)
            r0 = [pl.multiple_of((i if d == 0 else nc - 1 - i) * ck, ck) for d in dirs]
            cidx = [i, nc - 1 - i]
            cb16 = [cm_scr[pl.ds(r0[d], ck), :] for d in dirs]
            bt32 = [bt_scr[cidx[d]] for d in dirs]
            dt = [dts_scr[pl.ds(r0[d], ck), :] for d in dirs]
            dt_t = [dtt_scr[cidx[d]] for d in dirs]
            sp = [_split3(dt[d] * a_ref[0]) for d in dirs]
            sp_t = [_split3(dt_t[d] * acol_ref[0, 0]) for d in dirs]
            cum = [sum(jnp.dot(tri_bf[d], t, preferred_element_type=f32) for t in sp[d]) for d in dirs]
            cum_t = [sum(jnp.dot(t, tri_bf[1 - d], preferred_element_type=f32) for t in sp_t[d]) for d in dirs]
            cbm = [jnp.dot(cb16[d], bt32[d].astype(bf16), preferred_element_type=f32) for d in dirs]
            tot = [cum[0][ck - 1:ck, :], cum[1][0:1, :]]
            tot_t = [cum_t[0][:, ck - 1:ck], cum_t[1][:, 0:1]]
            w_t = [dt_t[d] * jnp.exp(tot_t[d] - cum_t[d]) for d in dirs]
            etot = [jnp.exp(tot[d]) for d in dirs]
            units = [(p, d) for p in range(SSD_KH // 2) for d in dirs]
            xp16 = {u: xs_scr[pl.ds(r0[u[1]], ck), u[0] * LANES:(u[0] + 1) * LANES] for u in units}
            hp = {u: h_scr[u[1] * (SSD_KH // 2) + u[0]] for u in units}
            ch = {u: jnp.dot(cb16[u[1]], hp[u].astype(bf16), preferred_element_type=f32) for u in units}
            mks, ccols, sts = {}, {}, {}
            for kk in range(2):
                for (p, d) in units:
                    j = d * SSD_KH + 2 * p + kk
                    ccols[p, d, kk] = jnp.broadcast_to(cum[d][:, j:j + 1], (ck, ck))
                    decay = jnp.exp(jnp.where(tri[d], ccols[p, d, kk] - cum_t[d][j:j + 1, :], NEG_INF))
                    mks[p, d, kk] = (cbm[d] * decay * dt_t[d][j:j + 1, :]).astype(bf16)
                    sts[p, d, kk] = jnp.dot((bt32[d] * w_t[d][j:j + 1, :]).astype(bf16), xp16[p, d],
                                            preferred_element_type=f32)
            for (p, d) in units:
                j0 = d * SSD_KH + 2 * p
                ydiag = jnp.where(first, jnp.dot(mks[p, d, 0], xp16[p, d], preferred_element_type=f32),
                                  jnp.dot(mks[p, d, 1], xp16[p, d], preferred_element_type=f32))
                yoff = jnp.exp(jnp.where(first, ccols[p, d, 0], ccols[p, d, 1])) * ch[p, d]
                ydir_scr[d, pl.ds(r0[d], ck), p * LANES:(p + 1) * LANES] = ydiag + yoff
                erow = jnp.where(first[0:1, :], etot[d][:, j0:j0 + 1], etot[d][:, j0 + 1:j0 + 2])
                h_scr[d * (SSD_KH // 2) + p] = hp[p, d] * erow + jnp.where(first, sts[p, d, 0], sts[p, d, 1])
            return carry

        lax.fori_loop(0, nc, chunk, 0)

        def gate(c, carry):
            r0 = pl.multiple_of(c * ck, ck)
            y = y_scr[pl.ds(r0, ck), :] + ydir_scr[0, pl.ds(r0, ck), :] + ydir_scr[1, pl.ds(r0, ck), :]
            u = y * _silu(z_ref[0, pl.ds(r0, ck), :].astype(f32))
            u = u * lax.rsqrt(jnp.mean(u * u, axis=-1, keepdims=True) + NORM_EPS)
            o_ref[0, pl.ds(r0, ck), :] = (u * ng_ref[...]).astype(bf16)
            return carry

        lax.fori_loop(0, nc, gate, 0)

    run_seq(lc, xc_ref, bc_ref, cc_ref, dtc_ref, zc_ref, oc_ref)
    run_seq(seq, xl_ref, bl_ref, cl_ref, dtl_ref, zl_ref, ol_ref)


def _ssd(xbc_c, dt_c, z_c, xbc_l, dt_l, z_l, cw, cb, dtb, a_neg, a_col, dsk, ng):
    b, lc, _ = xbc_c.shape
    seq = xbc_l.shape[1]
    gw, nst = SSD_GW, SSD_STATE
    xoff = SSD_W // nst

    def seq_specs(t):
        return [pl.BlockSpec((1, t, gw), lambda bi, g: (bi, 0, g)),
                pl.BlockSpec((1, t, nst), lambda bi, g: (bi, 0, xoff + g)),
                pl.BlockSpec((1, t, nst), lambda bi, g: (bi, 0, xoff + SSD_GROUPS + g)),
                pl.BlockSpec((1, t, LANES), lambda bi, g: (bi, 0, g)),
                pl.BlockSpec((1, t, gw), lambda bi, g: (bi, 0, g))]

    cwid = gw + 2 * nst
    par_specs = [pl.BlockSpec((1, 3, cwid), lambda bi, g: (g, 0, 0)),
                 pl.BlockSpec((1, 1, cwid), lambda bi, g: (g, 0, 0)),
                 pl.BlockSpec((1, 1, LANES), lambda bi, g: (g, 0, 0)),
                 pl.BlockSpec((1, 1, LANES), lambda bi, g: (g, 0, 0)),
                 pl.BlockSpec((1, 1, DT_ROWS, 1), lambda bi, g: (g, 0, 0, 0)),
                 pl.BlockSpec((1, gw), lambda bi, g: (0, g)),
                 pl.BlockSpec((1, gw), lambda bi, g: (0, g))]
    nck = seq // SSD_CHUNK
    return pl.pallas_call(
        functools.partial(_ssd_kernel, lc=lc, seq=seq),
        grid=(b, SSD_GROUPS),
        in_specs=seq_specs(lc) + seq_specs(seq) + par_specs,
        out_specs=[pl.BlockSpec((1, lc, gw), lambda bi, g: (bi, 0, g)),
                   pl.BlockSpec((1, seq, gw), lambda bi, g: (bi, 0, g))],
        out_shape=[jax.ShapeDtypeStruct((b, lc, SSD_W), bf16), jax.ShapeDtypeStruct((b, seq, SSD_W), bf16)],
        scratch_shapes=[pltpu.VMEM((seq + 16, cwid), f32),
                        pltpu.VMEM((seq, gw), bf16),
                        pltpu.VMEM((seq, nst), bf16),
                        pltpu.VMEM((nck, nst, SSD_CHUNK), f32),
                        pltpu.VMEM((seq, LANES), f32),
                        pltpu.VMEM((nck, DT_ROWS, SSD_CHUNK), f32),
                        pltpu.VMEM((seq, gw), f32),
                        pltpu.VMEM((2, seq, gw), f32),
                        pltpu.VMEM((SSD_KH, nst, LANES), f32)],
        compiler_params=_cparams(("arbitrary", "arbitrary")),
        name="ssd",
    )(xbc_c, xbc_c, xbc_c, dt_c, z_c, xbc_l, xbc_l, xbc_l, dt_l, z_l, cw, cb, dtb, a_neg, a_col, dsk, ng)


def _sconv_kernel(b_ref, c_ref, h_ref, w_ref, o_ref, pad_scr, *, t):
    ck = min(256, t)
    zero8 = jnp.zeros((8, LANES), f32)
    pad_scr[0:8, :] = zero8
    pad_scr[8 + t:16 + t, :] = zero8

    def fill(c, carry):
        r0 = pl.multiple_of(c * ck, ck)
        pad_scr[pl.ds(r0 + 8, ck), :] = c_ref[0, pl.ds(r0, ck), :].astype(f32) * h_ref[0, pl.ds(r0, ck), :].astype(f32)
        return carry

    lax.fori_loop(0, t // ck, fill, 0)

    def conv(c, carry):
        r0 = pl.multiple_of(c * ck, ck)
        xm, x0, xp = _shifted_rows(pad_scr, r0, ck)
        acc = w_ref[0:1, :] * xm + w_ref[1:2, :] * x0 + w_ref[2:3, :] * xp
        o_ref[0, pl.ds(r0, ck), :] = (b_ref[0, pl.ds(r0, ck), :].astype(f32) * acc).astype(bf16)
        return carry

    lax.fori_loop(0, t // ck, conv, 0)


def _sconv(scb, scc, sch, w):
    b, t, cw = scb.shape
    spec = pl.BlockSpec((1, t, LANES), lambda bi, j: (bi, 0, j))
    return pl.pallas_call(
        functools.partial(_sconv_kernel, t=t),
        grid=(b, cw // LANES),
        in_specs=[spec, spec, spec, pl.BlockSpec((3, LANES), lambda bi, j: (0, j))],
        out_specs=spec,
        out_shape=jax.ShapeDtypeStruct((b, t, cw), bf16),
        scratch_shapes=[pltpu.VMEM((t + 16, LANES), f32)],
        compiler_params=_cparams(("arbitrary", "arbitrary")),
        name="sconv",
    )(scb, scc, sch, w)


def _outproj_kernel(att_ref, ssd_ref, sc_ref, x_ref, g1_ref, sh2_ref, sc2_ref, n2_ref, wo_ref, wr_ref,
                    xo_ref, h2_ref, lg_ref, *, sub):
    for r0 in range(0, x_ref.shape[0], sub):
        rs = slice(r0, r0 + sub)
        acc = jnp.dot(att_ref[rs, :], wo_ref[0, 0:ATT_W, :], preferred_element_type=f32)
        acc = acc + jnp.dot(ssd_ref[rs, :], wo_ref[0, ATT_W:ATT_W + SSD_W, :], preferred_element_type=f32)
        acc = acc + jnp.dot(sc_ref[rs, :], wo_ref[0, ATT_W + SSD_W:, :], preferred_element_type=f32)
        x = x_ref[rs, :] + g1_ref[0] * acc
        xo_ref[rs, :] = x
        y = x * lax.rsqrt(jnp.mean(x * x, axis=-1, keepdims=True) + NORM_EPS) * n2_ref[...]
        h2 = y * (1.0 + sc2_ref[0]) + sh2_ref[0]
        hh = h2.astype(bf16)
        h2_ref[rs, :] = hh
        hm = (h2 - hh.astype(f32)).astype(bf16)
        r1 = jnp.dot(hh, wr_ref[0], preferred_element_type=f32)
        r2 = jnp.dot(hm, wr_ref[0, :, 0:ROUTER_PAD], preferred_element_type=f32)
        lg_ref[rs, :] = r1[:, 0:ROUTER_PAD] + r1[:, ROUTER_PAD:] + r2


def _outproj(att, ssd, sconv, x2d, g1, sh2, sc2, n2, wo, wr2, layer, tm, rows_per_mod):
    m, d = x2d.shape
    tpm = rows_per_mod // tm
    mod_spec = pl.BlockSpec((1, 1, d), lambda i: (i // tpm, 0, 0))
    return pl.pallas_call(
        functools.partial(_outproj_kernel, sub=min(256, tm)),
        grid=(m // tm,),
        in_specs=[pl.BlockSpec((tm, ATT_W), lambda i: (i, 0)), pl.BlockSpec((tm, SSD_W), lambda i: (i, 0)),
                  pl.BlockSpec((tm, SC_W), lambda i: (i, 0)), pl.BlockSpec((tm, d), lambda i: (i, 0)),
                  mod_spec, mod_spec, mod_spec, pl.BlockSpec((1, d), lambda i: (0, 0)),
                  pl.BlockSpec((1, d, d), lambda i: (layer, 0, 0)),
                  pl.BlockSpec((1, d, 2 * ROUTER_PAD), lambda i: (layer, 0, 0))],
        out_specs=[pl.BlockSpec((tm, d), lambda i: (i, 0)), pl.BlockSpec((tm, d), lambda i: (i, 0)),
                   pl.BlockSpec((tm, ROUTER_PAD), lambda i: (i, 0))],
        out_shape=[jax.ShapeDtypeStruct((m, d), f32), jax.ShapeDtypeStruct((m, d), bf16),
                   jax.ShapeDtypeStruct((m, ROUTER_PAD), f32)],
        compiler_params=_cparams(("arbitrary",)),
        name="outproj",
    )(att, ssd, sconv, x2d, g1, sh2, sc2, n2, wo, wr2)


def _ffn_kernel(*refs, nt, with_ctx):
    if with_ctx:
        x_ref, gv_ref, xc_ref, gvc_ref, wg_ref, wu_ref, wd_ref, o_ref, oc_ref = refs
    else:
        x_ref, gv_ref, wg_ref, wu_ref, wd_ref, o_ref = refs

    def ffn(xr, gr, outr):
        x = xr[0]
        a = jnp.dot(x, wg_ref[0, 0], preferred_element_type=f32)
        u = jnp.dot(x, wu_ref[0, 0], preferred_element_type=f32)
        hmid = (_silu(a) * u).astype(bf16)
        outr[0] = (jnp.dot(hmid, wd_ref[0, 0], preferred_element_type=f32) * gr[0]).astype(bf16)

    if not with_ctx:
        ffn(x_ref, gv_ref, o_ref)
        return
    i = pl.program_id(1)

    @pl.when(i < nt)
    def _():
        ffn(x_ref, gv_ref, o_ref)

    @pl.when(i == nt)
    def _():
        ffn(xc_ref, gvc_ref, oc_ref)


def _ffn_tile(nrows):
    for nt in range(1, nrows + 1):
        if nrows % nt == 0 and (nrows // nt) % 16 == 0 and nrows // nt <= FFN_MAX_TILE:
            return nrows // nt
    raise ValueError(f"no aligned row tile for {nrows} gathered rows")


def _expert_ffn(xg, gv, wg, wu, wd, layer, xgc=None, gvc=None):
    e, r, d = xg.shape
    ff = wg.shape[3]
    tm = _ffn_tile(r)
    nt = r // tm
    with_ctx = xgc is not None
    last = nt - 1
    row_specs = [pl.BlockSpec((1, tm, d), lambda ei, i: (ei, jnp.minimum(i, last), 0)),
                 pl.BlockSpec((1, tm, 1), lambda ei, i: (ei, jnp.minimum(i, last), 0))]
    w_specs = [pl.BlockSpec((1, 1, d, ff), lambda ei, i: (layer, ei, 0, 0)),
               pl.BlockSpec((1, 1, d, ff), lambda ei, i: (layer, ei, 0, 0)),
               pl.BlockSpec((1, 1, ff, d), lambda ei, i: (layer, ei, 0, 0))]
    out_specs = [pl.BlockSpec((1, tm, d), lambda ei, i: (ei, jnp.minimum(i, last), 0))]
    out_shape = [jax.ShapeDtypeStruct((e, r, d), bf16)]
    args = [xg, gv]
    if with_ctx:
        rc = xgc.shape[1]
        row_specs += [pl.BlockSpec((1, rc, d), lambda ei, i: (ei, 0, 0)),
                      pl.BlockSpec((1, rc, 1), lambda ei, i: (ei, 0, 0))]
        out_specs.append(pl.BlockSpec((1, rc, d), lambda ei, i: (ei, 0, 0)))
        out_shape.append(jax.ShapeDtypeStruct((e, rc, d), bf16))
        args += [xgc, gvc]
    outs = pl.pallas_call(
        functools.partial(_ffn_kernel, nt=nt, with_ctx=with_ctx),
        grid=(e, nt + (1 if with_ctx else 0)),
        in_specs=row_specs + w_specs,
        out_specs=out_specs,
        out_shape=out_shape,
        compiler_params=_cparams(("arbitrary", "arbitrary")),
        name="expert_ffn",
    )(*args, wg, wu, wd)
    return outs if with_ctx else outs[0]


def _combine_kernel(hit_ref, idx_ref, y_ref, xm_ref, g2_ref, o_ref, acc_scr, *, sb):
    bi = pl.program_id(0)
    k = pl.program_id(1)
    e = pl.program_id(2)
    nb, nk, ne = pl.num_programs(0), pl.num_programs(1), pl.num_programs(2)
    tt = acc_scr.shape[0]
    nsb = y_ref.shape[1] // sb

    @pl.when(e == 0)
    def _():
        acc_scr[...] = jnp.zeros(acc_scr.shape, f32)

    tok = lax.broadcasted_iota(jnp.int32, (tt, sb), 0) + k * tt
    for j in range(nsb):
        @pl.when(hit_ref[((e * nb + bi) * nk + k) * nsb + j] != 0)
        def _():
            onehot = jnp.where(tok == idx_ref[0, 0, :, j * sb:(j + 1) * sb], 1.0, 0.0).astype(bf16)
            acc_scr[...] += jnp.dot(onehot, y_ref[0, j * sb:(j + 1) * sb, :], preferred_element_type=f32)

    @pl.when(e == ne - 1)
    def _():
        o_ref[...] = xm_ref[...] + g2_ref[0] * acc_scr[...]


def _combine(idx, y, xm, g2, b, t, per_sample_gate):
    e, _, _, cap = idx.shape
    d = xm.shape[1]
    tt = min(COMBINE_TILE, t)
    nk = t // tt
    sb = min(COMBINE_SLOTS, cap)
    nsb = cap // sb
    blk = idx.reshape(e, b, nsb, sb)
    lo = blk[..., 0][:, :, None, :]
    hi = blk[..., sb - 1][:, :, None, :]
    tile0 = (jnp.arange(nk, dtype=idx.dtype) * tt)[None, None, :, None]
    hit = ((lo < tile0 + tt) & (hi >= tile0)).astype(jnp.int32).reshape(-1)
    grid_spec = pltpu.PrefetchScalarGridSpec(
        num_scalar_prefetch=1,
        grid=(b, nk, e),
        in_specs=[pl.BlockSpec((1, 1, 1, cap), lambda bi, k, ei, hit_ref: (ei, bi, 0, 0)),
                  pl.BlockSpec((1, cap, d), lambda bi, k, ei, hit_ref: (ei, bi, 0)),
                  pl.BlockSpec((tt, d), lambda bi, k, ei, hit_ref: (bi * nk + k, 0)),
                  pl.BlockSpec((1, 1, d), lambda bi, k, ei, hit_ref: (bi if per_sample_gate else 0, 0, 0))],
        out_specs=pl.BlockSpec((tt, d), lambda bi, k, ei, hit_ref: (bi * nk + k, 0)),
        scratch_shapes=[pltpu.VMEM((tt, d), f32)])
    return pl.pallas_call(
        functools.partial(_combine_kernel, sb=sb),
        grid_spec=grid_spec,
        out_shape=jax.ShapeDtypeStruct(xm.shape, f32),
        compiler_params=_cparams(("arbitrary", "arbitrary", "arbitrary")),
        name="moe_combine",
    )(hit, idx, y, xm, g2)


def _rope_tables(n_tokens):
    rows = n_tokens // GRID_W
    row = jnp.repeat(jnp.arange(rows), GRID_W).astype(f32)
    col = jnp.tile(jnp.arange(GRID_W), rows).astype(f32)
    n_freq = HEAD_DIM // 4
    inv = ROPE_THETA ** (-jnp.arange(n_freq, dtype=f32) / n_freq)
    ang_r = row[:, None] * inv
    ang_c = col[:, None] * inv
    cos = jnp.concatenate([jnp.cos(ang_r), jnp.cos(ang_r), jnp.cos(ang_c), jnp.cos(ang_c)], axis=-1)
    sins = jnp.concatenate([-jnp.sin(ang_r), jnp.sin(ang_r), -jnp.sin(ang_c), jnp.sin(ang_c)], axis=-1)
    return cos, sins


def _dt_layout(v):
    lead = v.shape[:-1]
    v = v.reshape(lead + (2, SSD_GROUPS, SSD_KH))
    v = jnp.moveaxis(v, -2, -3).reshape(lead + (SSD_GROUPS, 2 * SSD_KH))
    v = jnp.pad(v, [(0, 0)] * (len(lead) + 1) + [(0, LANES - 2 * SSD_KH)])
    return v.reshape(lead + (DT_PAD,))


def _group_layout(v):
    outs = []
    for g in range(SSD_GROUPS):
        outs.append(jnp.concatenate([
            v[..., g * SSD_GW:(g + 1) * SSD_GW],
            v[..., SSD_W + g * SSD_STATE:SSD_W + (g + 1) * SSD_STATE],
            v[..., SSD_W + SSD_BC_W + g * SSD_STATE:SSD_W + SSD_BC_W + (g + 1) * SSD_STATE]], axis=-1))
    return jnp.stack(outs)


def _route(logits, b, t):
    cap = EC_FACTOR * t // N_EXPERTS
    aff = jax.nn.softmax(logits[:, :N_EXPERTS].reshape(b, t, N_EXPERTS), axis=-1)
    gval, idx = lax.top_k(jnp.swapaxes(aff, 1, 2), cap)
    idx, gval = lax.sort((idx, gval), dimension=-1, num_keys=1)
    rows = idx + (jnp.arange(b, dtype=idx.dtype) * t)[:, None, None]
    to_e = lambda u: jnp.swapaxes(u, 0, 1).reshape(N_EXPERTS, b * cap)
    return to_e(gval)[..., None], to_e(rows), jnp.swapaxes(idx, 0, 1).reshape(N_EXPERTS, b, 1, cap)


def kernel(x, c, ctx, c_ctx, norm1_g, norm2_g, w_mod, b_mod, w_in, q_norm_g, k_norm_g, attn_sink, ssd_conv_w,
           ssd_conv_b, ssd_dt_bias, ssd_a_log, ssd_d, ssd_norm_g, sc_conv_w, w_out, w_router, w_expert_gate,
           w_expert_up, w_expert_down):
    b, s, d = x.shape
    lc = ctx.shape[1]
    nl = w_in.shape[0]
    cos, sins = _rope_tables(s)

    nrow = -(-(b + 1) // 8) * 8
    cvec = jnp.zeros((nrow, d), f32).at[:b].set(c).at[b].set(c_ctx)
    mods = _modulations(cvec, w_mod, b_mod).reshape(nl, nrow, N_MOD, 1, d)

    cuts = [0]
    for wdt in (ATT_W, ATT_KV_W, ATT_KV_W, SSD_W, SSD_XBC_W, SSD_DT_W, SC_W, SC_W, SC_W):
        cuts.append(cuts[-1] + wdt)
    w_dt = _dt_layout(w_in[:, :, cuts[5]:cuts[6]])
    w_in_p = jnp.concatenate([w_in[:, :, :cuts[5]], w_dt, w_in[:, :, cuts[6]:]], axis=-1).astype(bf16)
    w_out_b = w_out.astype(bf16)
    wr_pad = jnp.pad(w_router, ((0, 0), (0, 0), (0, ROUTER_PAD - N_EXPERTS)))
    wr_hi, wr_mid, _ = _split3(wr_pad)
    wr2 = jnp.concatenate([wr_hi, wr_mid], axis=-1)
    wg_b, wu_b, wd_b = w_expert_gate.astype(bf16), w_expert_up.astype(bf16), w_expert_down.astype(bf16)
    dtb = _dt_layout(ssd_dt_bias.reshape(nl, SSD_DT_W)).reshape(nl, SSD_GROUPS, 1, LANES)
    a_neg = _dt_layout(-jnp.exp(ssd_a_log.reshape(nl, SSD_DT_W))).reshape(nl, SSD_GROUPS, 1, LANES)
    a_col = a_neg[..., :DT_ROWS].reshape(nl, SSD_GROUPS, 1, DT_ROWS, 1)
    dsk = jnp.repeat(ssd_d, SSD_HEAD_DIM, axis=-1).reshape(nl, 1, SSD_W)

    xl = x.reshape(b * s, d)
    xc = ctx.reshape(b * lc, d)
    for i in range(nl):
        with_ctx_out = i < nl - 1
        sh1, sc1, g1, sh2, sc2, g2 = (mods[i, :, j] for j in range(N_MOD))
        n1 = norm1_g[i].reshape(1, d)
        n2 = norm2_g[i].reshape(1, d)
        qg = q_norm_g[i].reshape(1, HEAD_DIM)
        kg = k_norm_g[i].reshape(1, HEAD_DIM)

        q, k, v, z, xbc, dtr, scb, scc, sch = _inproj(xl, sh1, sc1, n1, w_in_p, i, 512, s)
        qc, kc, vc, zc, xbcc, dtrc, scbc, sccc, schc = _inproj(xc, sh1[b:], sc1[b:], n1, w_in_p, i, lc, b * lc)

        kp = _kprep(k, kg, cos, sins, s, True)
        kcp = _kprep(kc, kg, cos, sins, lc, False)
        r3 = lambda u, t: u.reshape(b, t, u.shape[-1])
        att = _attention(r3(q, s), qg, attn_sink[i], r3(kcp, lc), r3(vc, lc), r3(kp, s), r3(v, s), cos, sins)

        ssd_c, ssd_l = _ssd(r3(xbcc, lc), r3(dtrc, lc), r3(zc, lc), r3(xbc, s), r3(dtr, s), r3(z, s),
                            _group_layout(ssd_conv_w[i]), _group_layout(ssd_conv_b[i].reshape(1, -1)),
                            dtb[i], a_neg[i], a_col[i], dsk[i], ssd_norm_g[i].reshape(1, SSD_W))
        sconv = _sconv(r3(scb, s), r3(scc, s), r3(sch, s), sc_conv_w[i])

        xm, h2, lg = _outproj(att.reshape(b * s, ATT_W), ssd_l.reshape(b * s, SSD_W), sconv.reshape(b * s, SC_W),
                              xl, g1, sh2, sc2, n2, w_out_b, wr2, i, 512, s)
        gv, rows, idx = _route(lg, b, s)
        if with_ctx_out:
            attc = _attention(r3(qc, lc), qg, attn_sink[i], r3(kcp, lc), r3(vc, lc))
            sconvc = _sconv(r3(scbc, lc), r3(sccc, lc), r3(schc, lc), sc_conv_w[i])
            xmc, h2c, lgc = _outproj(attc.reshape(b * lc, ATT_W), ssd_c.reshape(b * lc, SSD_W),
                                     sconvc.reshape(b * lc, SC_W), xc, g1[b:], sh2[b:], sc2[b:], n2,
                                     w_out_b, wr2, i, lc, b * lc)
            gvc, rowsc, idxc = _route(lgc, b, lc)
            y, yc = _expert_ffn(h2[rows], gv, wg_b, wu_b, wd_b, i, h2c[rowsc], gvc)
            xc = _combine(idxc, yc, xmc, g2[b:], b, lc, False)
        else:
            y = _expert_ffn(h2[rows], gv, wg_b, wu_b, wd_b, i)
        xl = _combine(idx, y, xm, g2, b, s, True)
    return xl.reshape(b, s, d)
```

```python
import functools

import jax
import jax.numpy as jnp
from jax import lax
from jax.experimental import pallas as pl
from jax.experimental.pallas import tpu as pltpu

f32 = jnp.float32
bf16 = jnp.bfloat16

D_MODEL = 2048
DEPTH = 4
GRID_W = 64
NORM_EPS = 1e-6
N_MOD = 6
HEAD_DIM = 128
ATT_HEADS = 8
ATT_KV_HEADS = 2
ATT_GROUP = ATT_HEADS // ATT_KV_HEADS
ATT_WINDOW = 128
ATT_BLOCK = 128
ROPE_THETA = 10000.0
SSD_HEAD_DIM = 64
SSD_W = 512
SSD_HEADS = 8
SSD_GROUPS = 2
SSD_KH = SSD_HEADS // SSD_GROUPS
SSD_STATE = 128
SSD_CHUNK = 128
SSD_GW = SSD_W // SSD_GROUPS
SC_W = 512
ATT_W = ATT_HEADS * HEAD_DIM
ATT_KV_W = ATT_KV_HEADS * HEAD_DIM
SSD_BC_W = SSD_GROUPS * SSD_STATE
SSD_XBC_W = SSD_W + 2 * SSD_BC_W
SSD_DT_W = 2 * SSD_HEADS
N_EXPERTS = 16
EC_FACTOR = 2
EXPERT_FF = 1024

LANES = 128
DT_PAD = SSD_GROUPS * LANES
ROUTER_PAD = LANES
SEG_W = (ATT_W, ATT_KV_W, ATT_KV_W, SSD_W, SSD_XBC_W, DT_PAD, SC_W, SC_W, SC_W)
SEG_DT = (bf16, bf16, bf16, bf16, bf16, f32, bf16, bf16, bf16)
IN_W_PAD = sum(SEG_W)
VMEM_LIMIT = 58 * 1024 * 1024
NEG_INF = float("-inf")
LOG2E = 1.4426950408889634
ATT_TQ = 512
COMBINE_TILE = 1024
COMBINE_SLOTS = 256
DT_ROWS = 16
FFN_MAX_TILE = 576


def _cparams(sem):
    return pltpu.CompilerParams(dimension_semantics=sem, vmem_limit_bytes=VMEM_LIMIT)


def _silu(v):
    return v * jax.nn.sigmoid(v)


def _shifted_rows(pad_ref, r0, n):
    x0 = pad_ref[pl.ds(r0 + 8, n), :]
    before = pad_ref[pl.ds(r0, 8), :][7:8, :]
    after = pad_ref[pl.ds(r0 + 8 + n, 8), :][0:1, :]
    ri = lax.broadcasted_iota(jnp.int32, x0.shape, 0)
    xm = jnp.where(ri == 0, before, pltpu.roll(x0, 1, 0))
    xp = jnp.where(ri == n - 1, after, pltpu.roll(x0, n - 1, 0))
    return xm, x0, xp


def _split3(v):
    hi = v.astype(bf16)
    r = v - hi.astype(f32)
    mid = r.astype(bf16)
    lo = (r - mid.astype(f32)).astype(bf16)
    return hi, mid, lo


def _mod_kernel(c_ref, w_ref, b_ref, o_ref):
    a = _silu(c_ref[...]).astype(bf16)
    o_ref[0] = jnp.dot(a, w_ref[0].astype(bf16), preferred_element_type=f32) + b_ref[0]


def _modulations(cvec, w_mod, b_mod):
    nl, d, n = w_mod.shape
    r = cvec.shape[0]
    tn = 1024
    return pl.pallas_call(
        _mod_kernel,
        grid=(nl, n // tn),
        in_specs=[pl.BlockSpec((r, d), lambda l, j: (0, 0)),
                  pl.BlockSpec((1, d, tn), lambda l, j: (l, 0, j)),
                  pl.BlockSpec((1, 1, tn), lambda l, j: (l, 0, j))],
        out_specs=pl.BlockSpec((1, r, tn), lambda l, j: (l, 0, j)),
        out_shape=jax.ShapeDtypeStruct((nl, r, n), f32),
        compiler_params=_cparams(("arbitrary", "arbitrary")),
        name="modulations",
    )(cvec, w_mod, b_mod.reshape(nl, 1, n))


def _inproj_kernel(x_ref, sh_ref, sc_ref, g_ref, w_ref, *rest):
    outs, h_scr = rest[:-1], rest[-1]
    x = x_ref[...]
    y = x * lax.rsqrt(jnp.mean(x * x, axis=-1, keepdims=True) + NORM_EPS) * g_ref[...]
    h_scr[...] = (y * (1.0 + sc_ref[0]) + sh_ref[0]).astype(bf16)
    off = 0
    for ref, width in zip(outs, SEG_W):
        for c0 in range(0, width, 512):
            cw = min(512, width - c0)
            ref[:, c0:c0 + cw] = jnp.dot(h_scr[...], w_ref[0, :, off + c0:off + c0 + cw],
                                         preferred_element_type=f32).astype(ref.dtype)
        off += width


def _inproj(x2d, shift, scale, g, w, layer, tm, rows_per_mod):
    m, d = x2d.shape
    tpm = rows_per_mod // tm
    mod_spec = pl.BlockSpec((1, 1, d), lambda i: (i // tpm, 0, 0))
    return pl.pallas_call(
        _inproj_kernel,
        grid=(m // tm,),
        in_specs=[pl.BlockSpec((tm, d), lambda i: (i, 0)), mod_spec, mod_spec,
                  pl.BlockSpec((1, d), lambda i: (0, 0)),
                  pl.BlockSpec((1, d, IN_W_PAD), lambda i: (layer, 0, 0), pipeline_mode=pl.Buffered(1))],
        out_specs=[pl.BlockSpec((tm, wd), lambda i: (i, 0)) for wd in SEG_W],
        out_shape=[jax.ShapeDtypeStruct((m, wd), dt) for wd, dt in zip(SEG_W, SEG_DT)],
        scratch_shapes=[pltpu.VMEM((tm, d), bf16)],
        compiler_params=_cparams(("arbitrary",)),
        name="inproj",
    )(x2d, shift, scale, g, w)


def _norm_rope(v, g, cos, sins):
    y = v * lax.rsqrt(jnp.mean(v * v, axis=-1, keepdims=True) + NORM_EPS) * g
    if cos is None:
        return y
    lane = lax.broadcasted_iota(jnp.int32, y.shape, 1)
    quarter = HEAD_DIM // 4
    partner = jnp.where((lane % (2 * quarter)) < quarter,
                        pltpu.roll(y, HEAD_DIM - quarter, 1), pltpu.roll(y, quarter, 1))
    return y * cos + partner * sins


def _kprep_kernel(k_ref, g_ref, cos_ref, sin_ref, o_ref, *, rope):
    for hh in range(ATT_KV_HEADS):
        sl = slice(hh * HEAD_DIM, (hh + 1) * HEAD_DIM)
        v = k_ref[:, sl].astype(f32)
        o_ref[:, sl] = _norm_rope(v, g_ref[...], cos_ref[...] if rope else None,
                                  sin_ref[...] if rope else None).astype(bf16)


def _kprep(k2d, g, cos, sins, seq, rope):
    m = k2d.shape[0]
    tk = min(512, seq)
    nt = seq // tk
    tab = pl.BlockSpec((tk, HEAD_DIM), lambda i: (i % nt, 0))
    return pl.pallas_call(
        functools.partial(_kprep_kernel, rope=rope),
        grid=(m // tk,),
        in_specs=[pl.BlockSpec((tk, ATT_KV_W), lambda i: (i, 0)),
                  pl.BlockSpec((1, HEAD_DIM), lambda i: (0, 0)), tab, tab],
        out_specs=pl.BlockSpec((tk, ATT_KV_W), lambda i: (i, 0)),
        out_shape=jax.ShapeDtypeStruct((m, ATT_KV_W), bf16),
        compiler_params=_cparams(("arbitrary",)),
        name="kprep_rope" if rope else "kprep",
    )(k2d, g, cos, sins)


def _attn_kernel(*refs, band, tq, seq):
    if band:
        (sink_ref, q_ref, cos_ref, sin_ref, qg_ref, kp_ref, km_ref, kn_ref, vp_ref, vm_ref, vn_ref,
         kc_ref, vc_ref, o_ref) = refs
    else:
        sink_ref, q_ref, qg_ref, kc_ref, vc_ref, o_ref = refs
    h = pl.program_id(1)
    n = pl.program_id(2)
    qscale = HEAD_DIM ** -0.5 * LOG2E
    kc = kc_ref[0]
    vc = vc_ref[0]
    rows = ATT_GROUP * ATT_BLOCK
    if band:
        kwin = jnp.concatenate([kp_ref[0], km_ref[0], kn_ref[0]], axis=0)
        vwin = jnp.concatenate([vp_ref[0], vm_ref[0], vn_ref[0]], axis=0)
        ri = lax.broadcasted_iota(jnp.int32, (rows, 3 * ATT_BLOCK), 0) % ATT_BLOCK
        ci = lax.broadcasted_iota(jnp.int32, (rows, 3 * ATT_BLOCK), 1)
        band_bias = jnp.where(ci >= ri, jnp.where(ci <= ri + 2 * ATT_WINDOW, 0.0, NEG_INF), NEG_INF)
        col = lax.broadcasted_iota(jnp.int32, (1, 3 * ATT_BLOCK), 1)
    sinkcol = jnp.concatenate(
        [jnp.full((ATT_BLOCK, 1), sink_ref[h * ATT_GROUP + hh] * LOG2E, f32) for hh in range(ATT_GROUP)], axis=0)
    nt = (((1,), (1,)), ((), ()))
    blocks = range(tq // ATT_BLOCK)
    qs, s_c, s_w, m, p_c, p_w, den, o = {}, {}, {}, {}, {}, {}, {}, {}
    for jb in blocks:
        r0 = jb * ATT_BLOCK
        qparts = []
        for hh in range(ATT_GROUP):
            qv = q_ref[0, r0:r0 + ATT_BLOCK, hh * HEAD_DIM:(hh + 1) * HEAD_DIM].astype(f32)
            if band:
                qv = _norm_rope(qv, qg_ref[...], cos_ref[r0:r0 + ATT_BLOCK, :], sin_ref[r0:r0 + ATT_BLOCK, :])
            else:
                qv = _norm_rope(qv, qg_ref[...], None, None)
            qparts.append((qv * qscale).astype(bf16))
        qs[jb] = jnp.concatenate(qparts, axis=0)
    for jb in blocks:
        r0 = jb * ATT_BLOCK
        s_c[jb] = lax.dot_general(qs[jb], kc, nt, preferred_element_type=f32)
        if band:
            kpos = col + (n * tq + r0 - ATT_BLOCK)
            col_bias = jnp.where(kpos >= 0, jnp.where(kpos < seq, 0.0, NEG_INF), NEG_INF)
            s_w[jb] = (lax.dot_general(qs[jb], kwin[r0:r0 + 3 * ATT_BLOCK], nt, preferred_element_type=f32)
                       + (band_bias + col_bias))
    def lane_tiles(*arrs):
        return [a[:, c0:c0 + LANES] for a in arrs for c0 in range(0, a.shape[1], LANES)]

    nct = kc.shape[0] // LANES
    stiles = {}
    for jb in blocks:
        stiles[jb] = lane_tiles(s_c[jb], s_w[jb]) if band else lane_tiles(s_c[jb])
        m[jb] = jnp.maximum(jnp.max(functools.reduce(jnp.maximum, stiles[jb]), axis=-1, keepdims=True), sinkcol)
    for jb in blocks:
        mb = jnp.broadcast_to(m[jb], (rows, LANES))
        ptiles = [jnp.exp2(t - mb) for t in stiles[jb]]
        den[jb] = jnp.sum(functools.reduce(jnp.add, ptiles), axis=-1, keepdims=True) + jnp.exp2(sinkcol - m[jb])
        p_c[jb] = jnp.concatenate([t.astype(bf16) for t in ptiles[:nct]], axis=1)
        if band:
            p_w[jb] = jnp.concatenate([t.astype(bf16) for t in ptiles[nct:]], axis=1)
    for jb in blocks:
        r0 = jb * ATT_BLOCK
        o[jb] = jnp.dot(p_c[jb], vc, preferred_element_type=f32)
        if band:
            o[jb] = o[jb] + jnp.dot(p_w[jb], vwin[r0:r0 + 3 * ATT_BLOCK], preferred_element_type=f32)
    for jb in blocks:
        r0 = jb * ATT_BLOCK
        res = o[jb] * jnp.broadcast_to(1.0 / den[jb], (rows, HEAD_DIM))
        for hh in range(ATT_GROUP):
            o_ref[0, r0:r0 + ATT_BLOCK, hh * HEAD_DIM:(hh + 1) * HEAD_DIM] = (
                res[hh * ATT_BLOCK:(hh + 1) * ATT_BLOCK].astype(bf16))


def _attention(q, qg, sink, kc, vc, k=None, v=None, cos=None, sins=None):
    b, t, _ = q.shape
    lc = kc.shape[1]
    band = k is not None
    tq = ATT_TQ if t % ATT_TQ == 0 else 2 * ATT_BLOCK
    gw = ATT_GROUP * HEAD_DIM
    nblk = t // ATT_BLOCK
    per = tq // ATT_BLOCK
    smem = pl.BlockSpec(memory_space=pltpu.SMEM)
    qspec = pl.BlockSpec((1, tq, gw), lambda bi, h, n: (bi, n, h))
    gspec = pl.BlockSpec((1, HEAD_DIM), lambda bi, h, n: (0, 0))
    cspec = pl.BlockSpec((1, lc, HEAD_DIM), lambda bi, h, n: (bi, 0, h))
    if band:
        tab = pl.BlockSpec((tq, HEAD_DIM), lambda bi, h, n: (n, 0))
        prev = pl.BlockSpec((1, ATT_BLOCK, HEAD_DIM), lambda bi, h, n: (bi, jnp.maximum(n * per - 1, 0), h))
        main = pl.BlockSpec((1, tq, HEAD_DIM), lambda bi, h, n: (bi, n, h))
        nxt = pl.BlockSpec((1, ATT_BLOCK, HEAD_DIM), lambda bi, h, n: (bi, jnp.minimum((n + 1) * per, nblk - 1), h))
        in_specs = [smem, qspec, tab, tab, gspec, prev, main, nxt, prev, main, nxt, cspec, cspec]
        args = (sink, q, cos, sins, qg, k, k, k, v, v, v, kc, vc)
    else:
        in_specs = [smem, qspec, gspec, cspec, cspec]
        args = (sink, q, qg, kc, vc)
    return pl.pallas_call(
        functools.partial(_attn_kernel, band=band, tq=tq, seq=t),
        grid=(b, ATT_KV_HEADS, t // tq),
        in_specs=in_specs,
        out_specs=pl.BlockSpec((1, tq, gw), lambda bi, h, n: (bi, n, h)),
        out_shape=jax.ShapeDtypeStruct((b, t, ATT_W), bf16),
        compiler_params=_cparams(("arbitrary", "arbitrary", "arbitrary")),
        name="attn_band" if band else "attn_ctx",
    )(*args)


def _ssd_kernel(xc_ref, bc_ref, cc_ref, dtc_ref, zc_ref, xl_ref, bl_ref, cl_ref, dtl_ref, zl_ref,
                cw_ref, cb_ref, dtb_ref, a_ref, acol_ref, dsk_ref, ng_ref, oc_ref, ol_ref,
                pad_scr, xs_scr, cm_scr, bt_scr, dts_scr, dtt_scr, y_scr, ydir_scr, h_scr, *, lc, seq):
    ck = SSD_CHUNK
    gw = SSD_GW
    nst = SSD_STATE
    row = lax.broadcasted_iota(jnp.int32, (ck, ck), 0)
    col = lax.broadcasted_iota(jnp.int32, (ck, ck), 1)
    tri = (row >= col, col >= row)
    tri_bf = tuple(jnp.where(t, 1.0, 0.0).astype(bf16) for t in tri)
    first = col < SSD_HEAD_DIM
    h_scr[...] = jnp.zeros(h_scr.shape, f32)

    def run_seq(t, x_ref, b_ref, c_ref, dt_ref, z_ref, o_ref):
        nc = t // ck
        zero8 = jnp.zeros((8, gw + 2 * nst), f32)
        pad_scr[0:8, :] = zero8
        pad_scr[8 + t:16 + t, :] = zero8

        def fill(c, carry):
            r0 = pl.multiple_of(c * ck, ck)
            pad_scr[pl.ds(r0 + 8, ck), 0:gw] = x_ref[0, pl.ds(r0, ck), :].astype(f32)
            pad_scr[pl.ds(r0 + 8, ck), gw:gw + nst] = b_ref[0, pl.ds(r0, ck), :].astype(f32)
            pad_scr[pl.ds(r0 + 8, ck), gw + nst:gw + 2 * nst] = c_ref[0, pl.ds(r0, ck), :].astype(f32)
            return carry

        lax.fori_loop(0, nc, fill, 0)

        def conv(c, carry):
            r0 = pl.multiple_of(c * ck, ck)
            xm, x0, xp = _shifted_rows(pad_scr, r0, ck)
            act = _silu(cw_ref[0, 0:1, :] * xm + cw_ref[0, 1:2, :] * x0 + cw_ref[0, 2:3, :] * xp + cb_ref[0])
            xs = act[:, 0:gw]
            y_scr[pl.ds(r0, ck), :] = dsk_ref[...] * xs
            xs_scr[pl.ds(r0, ck), :] = xs.astype(bf16)
            bt_scr[c] = act[:, gw:gw + nst].T
            cm_scr[pl.ds(r0, ck), :] = act[:, gw + nst:gw + 2 * nst].astype(bf16)
            dv = dt_ref[0, pl.ds(r0, ck), :] + dtb_ref[0]
            dts = jnp.maximum(dv, 0.0) + jnp.log1p(jnp.exp(-jnp.abs(dv)))
            dts_scr[pl.ds(r0, ck), :] = dts
            dtt_scr[c] = dts.T[0:DT_ROWS, :]
            return carry

        lax.fori_loop(0, nc, conv, 0)

        def chunk(i, carry):
            dirs = (0, 1)
            r0 = [pl.multiple_of((i if d == 0 else nc - 1 - i) * ck, ck) for d in dirs]
            cidx = [i, nc - 1 - i]
            cb16 = [cm_scr[pl.ds(r0[d], ck), :] for d in dirs]
            bt32 = [bt_scr[cidx[d]] for d in dirs]
            dt = [dts_scr[pl.ds(r0[d], ck), :] for d in dirs]
            dt_t = [dtt_scr[cidx[d]] for d in dirs]
            sp = [_split3(dt[d] * a_ref[0]) for d in dirs]
            sp_t = [_split3(dt_t[d] * acol_ref[0, 0]) for d in dirs]
            c3 = [jnp.dot(tri_bf[d], jnp.concatenate(sp[d], axis=1), preferred_element_type=f32) for d in dirs]
            cum = [c3[d][:, 0:LANES] + c3[d][:, LANES:2 * LANES] + c3[d][:, 2 * LANES:] for d in dirs]
            c3t = [jnp.dot(jnp.concatenate(sp_t[d], axis=0), tri_bf[1 - d], preferred_element_type=f32) for d in dirs]
            cum_t = [c3t[d][0:DT_ROWS] + c3t[d][DT_ROWS:2 * DT_ROWS] + c3t[d][2 * DT_ROWS:] for d in dirs]
            cbm = [jnp.dot(cb16[d], bt32[d].astype(bf16), preferred_element_type=f32) for d in dirs]
            tot = [cum[0][ck - 1:ck, :], cum[1][0:1, :]]
            tot_t = [cum_t[0][:, ck - 1:ck], cum_t[1][:, 0:1]]
            w_t = [dt_t[d] * jnp.exp(tot_t[d] - cum_t[d]) for d in dirs]
            etot = [jnp.exp(tot[d]) for d in dirs]
            units = [(p, d) for p in range(SSD_KH // 2) for d in dirs]
            xp16 = {u: xs_scr[pl.ds(r0[u[1]], ck), u[0] * LANES:(u[0] + 1) * LANES] for u in units}
            hp = {u: h_scr[u[1] * (SSD_KH // 2) + u[0]] for u in units}
            ch = {u: jnp.dot(cb16[u[1]], hp[u].astype(bf16), preferred_element_type=f32) for u in units}
            mks, ccols, bws = {}, {}, {}
            for kk in range(2):
                for (p, d) in units:
                    j = d * SSD_KH + 2 * p + kk
                    ccols[p, d, kk] = jnp.broadcast_to(cum[d][:, j:j + 1], (ck, ck))
                    decay = jnp.exp(jnp.where(tri[d], ccols[p, d, kk] - cum_t[d][j:j + 1, :], NEG_INF))
                    mks[p, d, kk] = (cbm[d] * decay * dt_t[d][j:j + 1, :]).astype(bf16)
                    bws[p, d, kk] = (bt32[d] * w_t[d][j:j + 1, :]).astype(bf16)
            for (p, d) in units:
                j0 = d * SSD_KH + 2 * p
                yy = jnp.dot(jnp.concatenate([mks[p, d, 0], mks[p, d, 1]], axis=0), xp16[p, d],
                             preferred_element_type=f32)
                ss = jnp.dot(jnp.concatenate([bws[p, d, 0], bws[p, d, 1]], axis=0), xp16[p, d],
                             preferred_element_type=f32)
                ydiag = jnp.where(first, yy[0:ck], yy[ck:])
                yoff = jnp.exp(jnp.where(first, ccols[p, d, 0], ccols[p, d, 1])) * ch[p, d]
                ydir_scr[d, pl.ds(r0[d], ck), p * LANES:(p + 1) * LANES] = ydiag + yoff
                erow = jnp.where(first[0:1, :], etot[d][:, j0:j0 + 1], etot[d][:, j0 + 1:j0 + 2])
                h_scr[d * (SSD_KH // 2) + p] = hp[p, d] * erow + jnp.where(first, ss[0:nst], ss[nst:])
            return carry

        lax.fori_loop(0, nc, chunk, 0)

        def gate(c, carry):
            r0 = pl.multiple_of(c * ck, ck)
            y = y_scr[pl.ds(r0, ck), :] + ydir_scr[0, pl.ds(r0, ck), :] + ydir_scr[1, pl.ds(r0, ck), :]
            u = y * _silu(z_ref[0, pl.ds(r0, ck), :].astype(f32))
            u = u * lax.rsqrt(jnp.mean(u * u, axis=-1, keepdims=True) + NORM_EPS)
            o_ref[0, pl.ds(r0, ck), :] = (u * ng_ref[...]).astype(bf16)
            return carry

        lax.fori_loop(0, nc, gate, 0)

    run_seq(lc, xc_ref, bc_ref, cc_ref, dtc_ref, zc_ref, oc_ref)
    run_seq(seq, xl_ref, bl_ref, cl_ref, dtl_ref, zl_ref, ol_ref)


def _ssd(xbc_c, dt_c, z_c, xbc_l, dt_l, z_l, cw, cb, dtb, a_neg, a_col, dsk, ng):
    b, lc, _ = xbc_c.shape
    seq = xbc_l.shape[1]
    gw, nst = SSD_GW, SSD_STATE
    xoff = SSD_W // nst

    def seq_specs(t):
        return [pl.BlockSpec((1, t, gw), lambda bi, g: (bi, 0, g)),
                pl.BlockSpec((1, t, nst), lambda bi, g: (bi, 0, xoff + g)),
                pl.BlockSpec((1, t, nst), lambda bi, g: (bi, 0, xoff + SSD_GROUPS + g)),
                pl.BlockSpec((1, t, LANES), lambda bi, g: (bi, 0, g)),
                pl.BlockSpec((1, t, gw), lambda bi, g: (bi, 0, g))]

    cwid = gw + 2 * nst
    par_specs = [pl.BlockSpec((1, 3, cwid), lambda bi, g: (g, 0, 0)),
                 pl.BlockSpec((1, 1, cwid), lambda bi, g: (g, 0, 0)),
                 pl.BlockSpec((1, 1, LANES), lambda bi, g: (g, 0, 0)),
                 pl.BlockSpec((1, 1, LANES), lambda bi, g: (g, 0, 0)),
                 pl.BlockSpec((1, 1, DT_ROWS, 1), lambda bi, g: (g, 0, 0, 0)),
                 pl.BlockSpec((1, gw), lambda bi, g: (0, g)),
                 pl.BlockSpec((1, gw), lambda bi, g: (0, g))]
    nck = seq // SSD_CHUNK
    return pl.pallas_call(
        functools.partial(_ssd_kernel, lc=lc, seq=seq),
        grid=(b, SSD_GROUPS),
        in_specs=seq_specs(lc) + seq_specs(seq) + par_specs,
        out_specs=[pl.BlockSpec((1, lc, gw), lambda bi, g: (bi, 0, g)),
                   pl.BlockSpec((1, seq, gw), lambda bi, g: (bi, 0, g))],
        out_shape=[jax.ShapeDtypeStruct((b, lc, SSD_W), bf16), jax.ShapeDtypeStruct((b, seq, SSD_W), bf16)],
        scratch_shapes=[pltpu.VMEM((seq + 16, cwid), f32),
                        pltpu.VMEM((seq, gw), bf16),
                        pltpu.VMEM((seq, nst), bf16),
                        pltpu.VMEM((nck, nst, SSD_CHUNK), f32),
                        pltpu.VMEM((seq, LANES), f32),
                        pltpu.VMEM((nck, DT_ROWS, SSD_CHUNK), f32),
                        pltpu.VMEM((seq, gw), f32),
                        pltpu.VMEM((2, seq, gw), f32),
                        pltpu.VMEM((SSD_KH, nst, LANES), f32)],
        compiler_params=_cparams(("arbitrary", "arbitrary")),
        name="ssd",
    )(xbc_c, xbc_c, xbc_c, dt_c, z_c, xbc_l, xbc_l, xbc_l, dt_l, z_l, cw, cb, dtb, a_neg, a_col, dsk, ng)


def _sconv_kernel(b_ref, c_ref, h_ref, w_ref, o_ref, pad_scr, *, t):
    ck = min(256, t)
    zero8 = jnp.zeros((8, LANES), f32)
    pad_scr[0:8, :] = zero8
    pad_scr[8 + t:16 + t, :] = zero8

    def fill(c, carry):
        r0 = pl.multiple_of(c * ck, ck)
        pad_scr[pl.ds(r0 + 8, ck), :] = c_ref[0, pl.ds(r0, ck), :].astype(f32) * h_ref[0, pl.ds(r0, ck), :].astype(f32)
        return carry

    lax.fori_loop(0, t // ck, fill, 0)

    def conv(c, carry):
        r0 = pl.multiple_of(c * ck, ck)
        xm, x0, xp = _shifted_rows(pad_scr, r0, ck)
        acc = w_ref[0:1, :] * xm + w_ref[1:2, :] * x0 + w_ref[2:3, :] * xp
        o_ref[0, pl.ds(r0, ck), :] = (b_ref[0, pl.ds(r0, ck), :].astype(f32) * acc).astype(bf16)
        return carry

    lax.fori_loop(0, t // ck, conv, 0)


def _sconv(scb, scc, sch, w):
    b, t, cw = scb.shape
    spec = pl.BlockSpec((1, t, LANES), lambda bi, j: (bi, 0, j))
    return pl.pallas_call(
        functools.partial(_sconv_kernel, t=t),
        grid=(b, cw // LANES),
        in_specs=[spec, spec, spec, pl.BlockSpec((3, LANES), lambda bi, j: (0, j))],
        out_specs=spec,
        out_shape=jax.ShapeDtypeStruct((b, t, cw), bf16),
        scratch_shapes=[pltpu.VMEM((t + 16, LANES), f32)],
        compiler_params=_cparams(("arbitrary", "arbitrary")),
        name="sconv",
    )(scb, scc, sch, w)


def _outproj_kernel(att_ref, ssd_ref, sc_ref, x_ref, g1_ref, sh2_ref, sc2_ref, n2_ref, wo_ref, wr_ref,
                    xo_ref, h2_ref, lg_ref, *, sub):
    for r0 in range(0, x_ref.shape[0], sub):
        rs = slice(r0, r0 + sub)
        acc = jnp.dot(att_ref[rs, :], wo_ref[0, 0:ATT_W, :], preferred_element_type=f32)
        acc = acc + jnp.dot(ssd_ref[rs, :], wo_ref[0, ATT_W:ATT_W + SSD_W, :], preferred_element_type=f32)
        acc = acc + jnp.dot(sc_ref[rs, :], wo_ref[0, ATT_W + SSD_W:, :], preferred_element_type=f32)
        x = x_ref[rs, :] + g1_ref[0] * acc
        xo_ref[rs, :] = x
        y = x * lax.rsqrt(jnp.mean(x * x, axis=-1, keepdims=True) + NORM_EPS) * n2_ref[...]
        h2 = y * (1.0 + sc2_ref[0]) + sh2_ref[0]
        hh = h2.astype(bf16)
        h2_ref[rs, :] = hh
        hm = (h2 - hh.astype(f32)).astype(bf16)
        r1 = jnp.dot(hh, wr_ref[0], preferred_element_type=f32)
        r2 = jnp.dot(hm, wr_ref[0, :, 0:ROUTER_PAD], preferred_element_type=f32)
        lg_ref[rs, :] = r1[:, 0:ROUTER_PAD] + r1[:, ROUTER_PAD:] + r2


def _outproj(att, ssd, sconv, x2d, g1, sh2, sc2, n2, wo, wr2, layer, tm, rows_per_mod):
    m, d = x2d.shape
    tpm = rows_per_mod // tm
    mod_spec = pl.BlockSpec((1, 1, d), lambda i: (i // tpm, 0, 0))
    return pl.pallas_call(
        functools.partial(_outproj_kernel, sub=min(256, tm)),
        grid=(m // tm,),
        in_specs=[pl.BlockSpec((tm, ATT_W), lambda i: (i, 0)), pl.BlockSpec((tm, SSD_W), lambda i: (i, 0)),
                  pl.BlockSpec((tm, SC_W), lambda i: (i, 0)), pl.BlockSpec((tm, d), lambda i: (i, 0)),
                  mod_spec, mod_spec, mod_spec, pl.BlockSpec((1, d), lambda i: (0, 0)),
                  pl.BlockSpec((1, d, d), lambda i: (layer, 0, 0)),
                  pl.BlockSpec((1, d, 2 * ROUTER_PAD), lambda i: (layer, 0, 0))],
        out_specs=[pl.BlockSpec((tm, d), lambda i: (i, 0)), pl.BlockSpec((tm, d), lambda i: (i, 0)),
                   pl.BlockSpec((tm, ROUTER_PAD), lambda i: (i, 0))],
        out_shape=[jax.ShapeDtypeStruct((m, d), f32), jax.ShapeDtypeStruct((m, d), bf16),
                   jax.ShapeDtypeStruct((m, ROUTER_PAD), f32)],
        compiler_params=_cparams(("arbitrary",)),
        name="outproj",
    )(att, ssd, sconv, x2d, g1, sh2, sc2, n2, wo, wr2)


def _ffn_kernel(*refs, nt, with_ctx):
    if with_ctx:
        x_ref, gv_ref, xc_ref, gvc_ref, wg_ref, wu_ref, wd_ref, o_ref, oc_ref = refs
    else:
        x_ref, gv_ref, wg_ref, wu_ref, wd_ref, o_ref = refs

    def ffn(xr, gr, outr):
        x = xr[0]
        a = jnp.dot(x, wg_ref[0, 0].astype(bf16), preferred_element_type=f32)
        u = jnp.dot(x, wu_ref[0, 0].astype(bf16), preferred_element_type=f32)
        hmid = (_silu(a) * u).astype(bf16)
        outr[0] = (jnp.dot(hmid, wd_ref[0, 0].astype(bf16), preferred_element_type=f32) * gr[0]).astype(bf16)

    if not with_ctx:
        ffn(x_ref, gv_ref, o_ref)
        return
    i = pl.program_id(1)

    @pl.when(i < nt)
    def _():
        ffn(x_ref, gv_ref, o_ref)

    @pl.when(i == nt)
    def _():
        ffn(xc_ref, gvc_ref, oc_ref)


def _ffn_tile(nrows):
    for nt in range(1, nrows + 1):
        if nrows % nt == 0 and (nrows // nt) % 16 == 0 and nrows // nt <= FFN_MAX_TILE:
            return nrows // nt
    raise ValueError(f"no aligned row tile for {nrows} gathered rows")


def _expert_ffn(xg, gv, wg, wu, wd, layer, xgc=None, gvc=None):
    e, r, d = xg.shape
    ff = wg.shape[3]
    tm = _ffn_tile(r)
    nt = r // tm
    with_ctx = xgc is not None
    last = nt - 1
    row_specs = [pl.BlockSpec((1, tm, d), lambda ei, i: (ei, jnp.minimum(i, last), 0)),
                 pl.BlockSpec((1, tm, 1), lambda ei, i: (ei, jnp.minimum(i, last), 0))]
    one = pl.Buffered(1)
    w_specs = [pl.BlockSpec((1, 1, d, ff), lambda ei, i: (layer, ei, 0, 0), pipeline_mode=one),
               pl.BlockSpec((1, 1, d, ff), lambda ei, i: (layer, ei, 0, 0)),
               pl.BlockSpec((1, 1, ff, d), lambda ei, i: (layer, ei, 0, 0))]
    out_specs = [pl.BlockSpec((1, tm, d), lambda ei, i: (ei, jnp.minimum(i, last), 0))]
    out_shape = [jax.ShapeDtypeStruct((e, r, d), bf16)]
    args = [xg, gv]
    if with_ctx:
        rc = xgc.shape[1]
        row_specs += [pl.BlockSpec((1, rc, d), lambda ei, i: (ei, 0, 0)),
                      pl.BlockSpec((1, rc, 1), lambda ei, i: (ei, 0, 0))]
        out_specs.append(pl.BlockSpec((1, rc, d), lambda ei, i: (ei, 0, 0)))
        out_shape.append(jax.ShapeDtypeStruct((e, rc, d), bf16))
        args += [xgc, gvc]
    outs = pl.pallas_call(
        functools.partial(_ffn_kernel, nt=nt, with_ctx=with_ctx),
        grid=(e, nt + (1 if with_ctx else 0)),
        in_specs=row_specs + w_specs,
        out_specs=out_specs,
        out_shape=out_shape,
        compiler_params=_cparams(("arbitrary", "arbitrary")),
        name="expert_ffn",
    )(*args, wg, wu, wd)
    return outs if with_ctx else outs[0]


def _combine_kernel(hit_ref, idx_ref, y_ref, xm_ref, g2_ref, o_ref, acc_scr, *, sb):
    bi = pl.program_id(0)
    k = pl.program_id(1)
    e = pl.program_id(2)
    nb, nk, ne = pl.num_programs(0), pl.num_programs(1), pl.num_programs(2)
    tt = acc_scr.shape[0]
    nsb = y_ref.shape[1] // sb

    @pl.when(e == 0)
    def _():
        acc_scr[...] = jnp.zeros(acc_scr.shape, f32)

    tok = lax.broadcasted_iota(jnp.int32, (tt, sb), 0) + k * tt
    for j in range(nsb):
        @pl.when(hit_ref[((e * nb + bi) * nk + k) * nsb + j] != 0)
        def _():
            onehot = jnp.where(tok == idx_ref[0, 0, :, j * sb:(j + 1) * sb], 1.0, 0.0).astype(bf16)
            acc_scr[...] += jnp.dot(onehot, y_ref[0, j * sb:(j + 1) * sb, :], preferred_element_type=f32)

    @pl.when(e == ne - 1)
    def _():
        o_ref[...] = xm_ref[...] + g2_ref[0] * acc_scr[...]


def _combine(idx, y, xm, g2, b, t, per_sample_gate):
    e, _, _, cap = idx.shape
    d = xm.shape[1]
    tt = min(COMBINE_TILE, t)
    nk = t // tt
    sb = min(COMBINE_SLOTS, cap)
    nsb = cap // sb
    blk = idx.reshape(e, b, nsb, sb)
    lo = blk[..., 0][:, :, None, :]
    hi = blk[..., sb - 1][:, :, None, :]
    tile0 = (jnp.arange(nk, dtype=idx.dtype) * tt)[None, None, :, None]
    hit = ((lo < tile0 + tt) & (hi >= tile0)).astype(jnp.int32).reshape(-1)
    grid_spec = pltpu.PrefetchScalarGridSpec(
        num_scalar_prefetch=1,
        grid=(b, nk, e),
        in_specs=[pl.BlockSpec((1, 1, 1, cap), lambda bi, k, ei, hit_ref: (ei, bi, 0, 0)),
                  pl.BlockSpec((1, cap, d), lambda bi, k, ei, hit_ref: (ei, bi, 0)),
                  pl.BlockSpec((tt, d), lambda bi, k, ei, hit_ref: (bi * nk + k, 0)),
                  pl.BlockSpec((1, 1, d), lambda bi, k, ei, hit_ref: (bi if per_sample_gate else 0, 0, 0))],
        out_specs=pl.BlockSpec((tt, d), lambda bi, k, ei, hit_ref: (bi * nk + k, 0)),
        scratch_shapes=[pltpu.VMEM((tt, d), f32)])
    return pl.pallas_call(
        functools.partial(_combine_kernel, sb=sb),
        grid_spec=grid_spec,
        out_shape=jax.ShapeDtypeStruct(xm.shape, f32),
        compiler_params=_cparams(("arbitrary", "arbitrary", "arbitrary")),
        name="moe_combine",
    )(hit, idx, y, xm, g2)


def _rope_tables(n_tokens):
    rows = n_tokens // GRID_W
    row = jnp.repeat(jnp.arange(rows), GRID_W).astype(f32)
    col = jnp.tile(jnp.arange(GRID_W), rows).astype(f32)
    n_freq = HEAD_DIM // 4
    inv = ROPE_THETA ** (-jnp.arange(n_freq, dtype=f32) / n_freq)
    ang_r = row[:, None] * inv
    ang_c = col[:, None] * inv
    cos = jnp.concatenate([jnp.cos(ang_r), jnp.cos(ang_r), jnp.cos(ang_c), jnp.cos(ang_c)], axis=-1)
    sins = jnp.concatenate([-jnp.sin(ang_r), jnp.sin(ang_r), -jnp.sin(ang_c), jnp.sin(ang_c)], axis=-1)
    return cos, sins


def _dt_layout(v):
    lead = v.shape[:-1]
    v = v.reshape(lead + (2, SSD_GROUPS, SSD_KH))
    v = jnp.moveaxis(v, -2, -3).reshape(lead + (SSD_GROUPS, 2 * SSD_KH))
    v = jnp.pad(v, [(0, 0)] * (len(lead) + 1) + [(0, LANES - 2 * SSD_KH)])
    return v.reshape(lead + (DT_PAD,))


def _group_layout(v):
    outs = []
    for g in range(SSD_GROUPS):
        outs.append(jnp.concatenate([
            v[..., g * SSD_GW:(g + 1) * SSD_GW],
            v[..., SSD_W + g * SSD_STATE:SSD_W + (g + 1) * SSD_STATE],
            v[..., SSD_W + SSD_BC_W + g * SSD_STATE:SSD_W + SSD_BC_W + (g + 1) * SSD_STATE]], axis=-1))
    return jnp.stack(outs)


def _route(logits, b, t):
    cap = EC_FACTOR * t // N_EXPERTS
    aff = jax.nn.softmax(logits[:, :N_EXPERTS].reshape(b, t, N_EXPERTS), axis=-1)
    gval, idx = lax.top_k(jnp.swapaxes(aff, 1, 2), cap)
    idx, gval = lax.sort((idx, gval), dimension=-1, num_keys=1)
    rows = idx + (jnp.arange(b, dtype=idx.dtype) * t)[:, None, None]
    to_e = lambda u: jnp.swapaxes(u, 0, 1).reshape(N_EXPERTS, b * cap)
    return to_e(gval)[..., None], to_e(rows), jnp.swapaxes(idx, 0, 1).reshape(N_EXPERTS, b, 1, cap)


def kernel(x, c, ctx, c_ctx, norm1_g, norm2_g, w_mod, b_mod, w_in, q_norm_g, k_norm_g, attn_sink, ssd_conv_w,
           ssd_conv_b, ssd_dt_bias, ssd_a_log, ssd_d, ssd_norm_g, sc_conv_w, w_out, w_router, w_expert_gate,
           w_expert_up, w_expert_down):
    b, s, d = x.shape
    lc = ctx.shape[1]
    nl = w_in.shape[0]
    cos, sins = _rope_tables(s)

    nrow = -(-(b + 1) // 8) * 8
    cvec = jnp.zeros((nrow, d), f32).at[:b].set(c).at[b].set(c_ctx)
    mods = _modulations(cvec, w_mod, b_mod).reshape(nl, nrow, N_MOD, 1, d)

    cuts = [0]
    for wdt in (ATT_W, ATT_KV_W, ATT_KV_W, SSD_W, SSD_XBC_W, SSD_DT_W, SC_W, SC_W, SC_W):
        cuts.append(cuts[-1] + wdt)
    w_dt = _dt_layout(w_in[:, :, cuts[5]:cuts[6]])
    w_in_p = jnp.concatenate([w_in[:, :, :cuts[5]], w_dt, w_in[:, :, cuts[6]:]], axis=-1).astype(bf16)
    w_out_b = w_out.astype(bf16)
    wr_pad = jnp.pad(w_router, ((0, 0), (0, 0), (0, ROUTER_PAD - N_EXPERTS)))
    wr_hi, wr_mid, _ = _split3(wr_pad)
    wr2 = jnp.concatenate([wr_hi, wr_mid], axis=-1)
    dtb = _dt_layout(ssd_dt_bias.reshape(nl, SSD_DT_W)).reshape(nl, SSD_GROUPS, 1, LANES)
    a_neg = _dt_layout(-jnp.exp(ssd_a_log.reshape(nl, SSD_DT_W))).reshape(nl, SSD_GROUPS, 1, LANES)
    a_col = a_neg[..., :DT_ROWS].reshape(nl, SSD_GROUPS, 1, DT_ROWS, 1)
    dsk = jnp.repeat(ssd_d, SSD_HEAD_DIM, axis=-1).reshape(nl, 1, SSD_W)

    xl = x.reshape(b * s, d)
    xc = ctx.reshape(b * lc, d)
    for i in range(nl):
        with_ctx_out = i < nl - 1
        sh1, sc1, g1, sh2, sc2, g2 = (mods[i, :, j] for j in range(N_MOD))
        n1 = norm1_g[i].reshape(1, d)
        n2 = norm2_g[i].reshape(1, d)
        qg = q_norm_g[i].reshape(1, HEAD_DIM)
        kg = k_norm_g[i].reshape(1, HEAD_DIM)

        q, k, v, z, xbc, dtr, scb, scc, sch = _inproj(xl, sh1, sc1, n1, w_in_p, i, 512, s)
        qc, kc, vc, zc, xbcc, dtrc, scbc, sccc, schc = _inproj(xc, sh1[b:], sc1[b:], n1, w_in_p, i, lc, b * lc)

        kp = _kprep(k, kg, cos, sins, s, True)
        kcp = _kprep(kc, kg, cos, sins, lc, False)
        r3 = lambda u, t: u.reshape(b, t, u.shape[-1])
        att = _attention(r3(q, s), qg, attn_sink[i], r3(kcp, lc), r3(vc, lc), r3(kp, s), r3(v, s), cos, sins)

        ssd_c, ssd_l = _ssd(r3(xbcc, lc), r3(dtrc, lc), r3(zc, lc), r3(xbc, s), r3(dtr, s), r3(z, s),
                            _group_layout(ssd_conv_w[i]), _group_layout(ssd_conv_b[i].reshape(1, -1)),
                            dtb[i], a_neg[i], a_col[i], dsk[i], ssd_norm_g[i].reshape(1, SSD_W))
        sconv = _sconv(r3(scb, s), r3(scc, s), r3(sch, s), sc_conv_w[i])

        xm, h2, lg = _outproj(att.reshape(b * s, ATT_W), ssd_l.reshape(b * s, SSD_W), sconv.reshape(b * s, SC_W),
                              xl, g1, sh2, sc2, n2, w_out_b, wr2, i, 512, s)
        gv, rows, idx = _route(lg, b, s)
        if with_ctx_out:
            attc = _attention(r3(qc, lc), qg, attn_sink[i], r3(kcp, lc), r3(vc, lc))
            sconvc = _sconv(r3(scbc, lc), r3(sccc, lc), r3(schc, lc), sc_conv_w[i])
            xmc, h2c, lgc = _outproj(attc.reshape(b * lc, ATT_W), ssd_c.reshape(b * lc, SSD_W),
                                     sconvc.reshape(b * lc, SC_W), xc, g1[b:], sh2[b:], sc2[b:], n2,
                                     w_out_b, wr2, i, lc, b * lc)
            gvc, rowsc, idxc = _route(lgc, b, lc)
            y, yc = _expert_ffn(h2[rows], gv, w_expert_gate, w_expert_up, w_expert_down, i, h2c[rowsc], gvc)
            xc = _combine(idxc, yc, xmc, g2[b:], b, lc, False)
        else:
            y = _expert_ffn(h2[rows], gv, w_expert_gate, w_expert_up, w_expert_down, i)
        xl = _combine(idx, y, xm, g2, b, s, True)
    return xl.reshape(b, s, d)
```

```python
import functools

import jax
import jax.numpy as jnp
from jax import lax
from jax.experimental import pallas as pl
from jax.experimental.pallas import tpu as pltpu

f32 = jnp.float32
bf16 = jnp.bfloat16

D_MODEL = 2048
DEPTH = 4
GRID_W = 64
NORM_EPS = 1e-6
N_MOD = 6
HEAD_DIM = 128
ATT_HEADS = 8
ATT_KV_HEADS = 2
ATT_GROUP = ATT_HEADS // ATT_KV_HEADS
ATT_WINDOW = 128
ATT_BLOCK = 128
ROPE_THETA = 10000.0
SSD_HEAD_DIM = 64
SSD_W = 512
SSD_HEADS = 8
SSD_GROUPS = 2
SSD_KH = SSD_HEADS // SSD_GROUPS
SSD_STATE = 128
SSD_CHUNK = 128
SSD_GW = SSD_W // SSD_GROUPS
SC_W = 512
ATT_W = ATT_HEADS * HEAD_DIM
ATT_KV_W = ATT_KV_HEADS * HEAD_DIM
SSD_BC_W = SSD_GROUPS * SSD_STATE
SSD_XBC_W = SSD_W + 2 * SSD_BC_W
SSD_DT_W = 2 * SSD_HEADS
N_EXPERTS = 16
EC_FACTOR = 2
EXPERT_FF = 1024

LANES = 128
DT_PAD = SSD_GROUPS * LANES
ROUTER_PAD = LANES
SEG_W = (ATT_W, ATT_KV_W, ATT_KV_W, SSD_W, SSD_XBC_W, DT_PAD, SC_W, SC_W, SC_W)
SEG_DT = (bf16, bf16, bf16, bf16, bf16, f32, bf16, bf16, bf16)
IN_W_PAD = sum(SEG_W)
VMEM_LIMIT = 58 * 1024 * 1024
NEG_INF = float("-inf")
LOG2E = 1.4426950408889634
ATT_TQ = 1024
ATT_GROUP_BLOCKS = 8
COMBINE_TILE = 1024
COMBINE_SLOTS = 256
DT_ROWS = 16
SSD_TRIP_CHUNKS = 2
FFN_MAX_TILE = 576


def _cparams(sem):
    return pltpu.CompilerParams(dimension_semantics=sem, vmem_limit_bytes=VMEM_LIMIT)


def _silu(v):
    return v * jax.nn.sigmoid(v)


def _shifted_rows(pad_ref, r0, n):
    x0 = pad_ref[pl.ds(r0 + 8, n), :]
    before = pad_ref[pl.ds(r0, 8), :][7:8, :]
    after = pad_ref[pl.ds(r0 + 8 + n, 8), :][0:1, :]
    ri = lax.broadcasted_iota(jnp.int32, x0.shape, 0)
    xm = jnp.where(ri == 0, before, pltpu.roll(x0, 1, 0))
    xp = jnp.where(ri == n - 1, after, pltpu.roll(x0, n - 1, 0))
    return xm, x0, xp


def _split3(v):
    hi = v.astype(bf16)
    r = v - hi.astype(f32)
    mid = r.astype(bf16)
    lo = (r - mid.astype(f32)).astype(bf16)
    return hi, mid, lo


def _mod_kernel(c_ref, w_ref, b_ref, o_ref):
    a = _silu(c_ref[...]).astype(bf16)
    o_ref[0] = jnp.dot(a, w_ref[0].astype(bf16), preferred_element_type=f32) + b_ref[0]


def _modulations(cvec, w_mod, b_mod):
    nl, d, n = w_mod.shape
    r = cvec.shape[0]
    tn = 1024
    return pl.pallas_call(
        _mod_kernel,
        grid=(nl, n // tn),
        in_specs=[pl.BlockSpec((r, d), lambda l, j: (0, 0)),
                  pl.BlockSpec((1, d, tn), lambda l, j: (l, 0, j)),
                  pl.BlockSpec((1, 1, tn), lambda l, j: (l, 0, j))],
        out_specs=pl.BlockSpec((1, r, tn), lambda l, j: (l, 0, j)),
        out_shape=jax.ShapeDtypeStruct((nl, r, n), f32),
        compiler_params=_cparams(("arbitrary", "arbitrary")),
        name="modulations",
    )(cvec, w_mod, b_mod.reshape(nl, 1, n))


def _inproj_kernel(x_ref, sh_ref, sc_ref, g_ref, w_ref, *rest):
    outs, h_scr = rest[:-1], rest[-1]
    x = x_ref[...]
    y = x * lax.rsqrt(jnp.mean(x * x, axis=-1, keepdims=True) + NORM_EPS) * g_ref[...]
    h_scr[...] = (y * (1.0 + sc_ref[0]) + sh_ref[0]).astype(bf16)
    off = 0
    for ref, width in zip(outs, SEG_W):
        for c0 in range(0, width, 512):
            cw = min(512, width - c0)
            ref[:, c0:c0 + cw] = jnp.dot(h_scr[...], w_ref[0, :, off + c0:off + c0 + cw],
                                         preferred_element_type=f32).astype(ref.dtype)
        off += width


def _inproj(x2d, shift, scale, g, w, layer, tm, rows_per_mod):
    m, d = x2d.shape
    tpm = rows_per_mod // tm
    mod_spec = pl.BlockSpec((1, 1, d), lambda i: (i // tpm, 0, 0))
    return pl.pallas_call(
        _inproj_kernel,
        grid=(m // tm,),
        in_specs=[pl.BlockSpec((tm, d), lambda i: (i, 0)), mod_spec, mod_spec,
                  pl.BlockSpec((1, d), lambda i: (0, 0)),
                  pl.BlockSpec((1, d, IN_W_PAD), lambda i: (layer, 0, 0), pipeline_mode=pl.Buffered(1))],
        out_specs=[pl.BlockSpec((tm, wd), lambda i: (i, 0)) for wd in SEG_W],
        out_shape=[jax.ShapeDtypeStruct((m, wd), dt) for wd, dt in zip(SEG_W, SEG_DT)],
        scratch_shapes=[pltpu.VMEM((tm, d), bf16)],
        compiler_params=_cparams(("arbitrary",)),
        name="inproj",
    )(x2d, shift, scale, g, w)


def _norm_rope(v, g, cos, sins):
    y = v * lax.rsqrt(jnp.mean(v * v, axis=-1, keepdims=True) + NORM_EPS) * g
    if cos is None:
        return y
    lane = lax.broadcasted_iota(jnp.int32, y.shape, 1)
    quarter = HEAD_DIM // 4
    partner = jnp.where((lane % (2 * quarter)) < quarter,
                        pltpu.roll(y, HEAD_DIM - quarter, 1), pltpu.roll(y, quarter, 1))
    return y * cos + partner * sins


def _kprep_kernel(k_ref, g_ref, cos_ref, sin_ref, o_ref, *, rope):
    for hh in range(ATT_KV_HEADS):
        sl = slice(hh * HEAD_DIM, (hh + 1) * HEAD_DIM)
        v = k_ref[:, sl].astype(f32)
        o_ref[:, sl] = _norm_rope(v, g_ref[...], cos_ref[...] if rope else None,
                                  sin_ref[...] if rope else None).astype(bf16)


def _kprep(k2d, g, cos, sins, seq, rope):
    m = k2d.shape[0]
    tk = min(512, seq)
    nt = seq // tk
    tab = pl.BlockSpec((tk, HEAD_DIM), lambda i: (i % nt, 0))
    return pl.pallas_call(
        functools.partial(_kprep_kernel, rope=rope),
        grid=(m // tk,),
        in_specs=[pl.BlockSpec((tk, ATT_KV_W), lambda i: (i, 0)),
                  pl.BlockSpec((1, HEAD_DIM), lambda i: (0, 0)), tab, tab],
        out_specs=pl.BlockSpec((tk, ATT_KV_W), lambda i: (i, 0)),
        out_shape=jax.ShapeDtypeStruct((m, ATT_KV_W), bf16),
        compiler_params=_cparams(("arbitrary",)),
        name="kprep_rope" if rope else "kprep",
    )(k2d, g, cos, sins)


def _attn_kernel(*refs, band, tq, seq):
    if band:
        (sink_ref, q_ref, cos_ref, sin_ref, qg_ref, kp_ref, km_ref, kn_ref, vp_ref, vm_ref, vn_ref,
         kc_ref, vc_ref, o_ref) = refs
    else:
        sink_ref, q_ref, qg_ref, kc_ref, vc_ref, o_ref = refs
    h = pl.program_id(1)
    n = pl.program_id(2)
    qscale = HEAD_DIM ** -0.5 * LOG2E
    kc = kc_ref[0]
    vc = vc_ref[0]
    rows = ATT_GROUP * ATT_BLOCK
    if band:
        kwin = jnp.concatenate([kp_ref[0], km_ref[0], kn_ref[0]], axis=0)
        vwin = jnp.concatenate([vp_ref[0], vm_ref[0], vn_ref[0]], axis=0)
        ri = lax.broadcasted_iota(jnp.int32, (rows, 3 * ATT_BLOCK), 0) % ATT_BLOCK
        ci = lax.broadcasted_iota(jnp.int32, (rows, 3 * ATT_BLOCK), 1)
        band_bias = jnp.where(ci >= ri, jnp.where(ci <= ri + 2 * ATT_WINDOW, 0.0, NEG_INF), NEG_INF)
        col = lax.broadcasted_iota(jnp.int32, (1, 3 * ATT_BLOCK), 1)
    sinkcol = jnp.concatenate(
        [jnp.full((ATT_BLOCK, 1), sink_ref[h * ATT_GROUP + hh] * LOG2E, f32) for hh in range(ATT_GROUP)], axis=0)
    nt = (((1,), (1,)), ((), ()))
    nblocks = tq // ATT_BLOCK
    for g0 in range(0, nblocks, ATT_GROUP_BLOCKS):
        _attn_blocks(range(g0, min(g0 + ATT_GROUP_BLOCKS, nblocks)), locals())


def _attn_blocks(blocks, env):
    band, tq, seq, n, rows, nt, qscale = (env[k] for k in ("band", "tq", "seq", "n", "rows", "nt", "qscale"))
    q_ref, qg_ref, o_ref, kc, vc, sinkcol = (env[k] for k in ("q_ref", "qg_ref", "o_ref", "kc", "vc", "sinkcol"))
    if band:
        cos_ref, sin_ref, kwin, vwin, band_bias, col = (
            env[k] for k in ("cos_ref", "sin_ref", "kwin", "vwin", "band_bias", "col"))
    qs, s_c, s_w, m, p_c, p_w, den, o = {}, {}, {}, {}, {}, {}, {}, {}
    for jb in blocks:
        r0 = jb * ATT_BLOCK
        qparts = []
        for hh in range(ATT_GROUP):
            qv = q_ref[0, r0:r0 + ATT_BLOCK, hh * HEAD_DIM:(hh + 1) * HEAD_DIM].astype(f32)
            if band:
                qv = _norm_rope(qv, qg_ref[...], cos_ref[r0:r0 + ATT_BLOCK, :], sin_ref[r0:r0 + ATT_BLOCK, :])
            else:
                qv = _norm_rope(qv, qg_ref[...], None, None)
            qparts.append((qv * qscale).astype(bf16))
        qs[jb] = jnp.concatenate(qparts, axis=0)
    for jb in blocks:
        r0 = jb * ATT_BLOCK
        s_c[jb] = lax.dot_general(qs[jb], kc, nt, preferred_element_type=f32)
        if band:
            kpos = col + (n * tq + r0 - ATT_BLOCK)
            col_bias = jnp.where(kpos >= 0, jnp.where(kpos < seq, 0.0, NEG_INF), NEG_INF)
            s_w[jb] = (lax.dot_general(qs[jb], kwin[r0:r0 + 3 * ATT_BLOCK], nt, preferred_element_type=f32)
                       + (band_bias + col_bias))
    def lane_tiles(*arrs):
        return [a[:, c0:c0 + LANES] for a in arrs for c0 in range(0, a.shape[1], LANES)]

    nct = kc.shape[0] // LANES
    stiles = {}
    for jb in blocks:
        stiles[jb] = lane_tiles(s_c[jb], s_w[jb]) if band else lane_tiles(s_c[jb])
        m[jb] = jnp.maximum(jnp.max(functools.reduce(jnp.maximum, stiles[jb]), axis=-1, keepdims=True), sinkcol)
    for jb in blocks:
        mb = jnp.broadcast_to(m[jb], (rows, LANES))
        ptiles = [jnp.exp2(t - mb) for t in stiles[jb]]
        den[jb] = jnp.sum(functools.reduce(jnp.add, ptiles), axis=-1, keepdims=True) + jnp.exp2(sinkcol - m[jb])
        p_c[jb] = jnp.concatenate([t.astype(bf16) for t in ptiles[:nct]], axis=1)
        if band:
            p_w[jb] = jnp.concatenate([t.astype(bf16) for t in ptiles[nct:]], axis=1)
    for jb in blocks:
        r0 = jb * ATT_BLOCK
        o[jb] = jnp.dot(p_c[jb], vc, preferred_element_type=f32)
        if band:
            o[jb] = o[jb] + jnp.dot(p_w[jb], vwin[r0:r0 + 3 * ATT_BLOCK], preferred_element_type=f32)
    for jb in blocks:
        r0 = jb * ATT_BLOCK
        res = o[jb] * jnp.broadcast_to(1.0 / den[jb], (rows, HEAD_DIM))
        for hh in range(ATT_GROUP):
            o_ref[0, r0:r0 + ATT_BLOCK, hh * HEAD_DIM:(hh + 1) * HEAD_DIM] = (
                res[hh * ATT_BLOCK:(hh + 1) * ATT_BLOCK].astype(bf16))


def _attention(q, qg, sink, kc, vc, k=None, v=None, cos=None, sins=None):
    b, t, _ = q.shape
    lc = kc.shape[1]
    band = k is not None
    tq = ATT_TQ if t % ATT_TQ == 0 else 2 * ATT_BLOCK
    gw = ATT_GROUP * HEAD_DIM
    nblk = t // ATT_BLOCK
    per = tq // ATT_BLOCK
    smem = pl.BlockSpec(memory_space=pltpu.SMEM)
    qspec = pl.BlockSpec((1, tq, gw), lambda bi, h, n: (bi, n, h))
    gspec = pl.BlockSpec((1, HEAD_DIM), lambda bi, h, n: (0, 0))
    cspec = pl.BlockSpec((1, lc, HEAD_DIM), lambda bi, h, n: (bi, 0, h))
    if band:
        tab = pl.BlockSpec((tq, HEAD_DIM), lambda bi, h, n: (n, 0))
        prev = pl.BlockSpec((1, ATT_BLOCK, HEAD_DIM), lambda bi, h, n: (bi, jnp.maximum(n * per - 1, 0), h))
        main = pl.BlockSpec((1, tq, HEAD_DIM), lambda bi, h, n: (bi, n, h))
        nxt = pl.BlockSpec((1, ATT_BLOCK, HEAD_DIM), lambda bi, h, n: (bi, jnp.minimum((n + 1) * per, nblk - 1), h))
        in_specs = [smem, qspec, tab, tab, gspec, prev, main, nxt, prev, main, nxt, cspec, cspec]
        args = (sink, q, cos, sins, qg, k, k, k, v, v, v, kc, vc)
    else:
        in_specs = [smem, qspec, gspec, cspec, cspec]
        args = (sink, q, qg, kc, vc)
    return pl.pallas_call(
        functools.partial(_attn_kernel, band=band, tq=tq, seq=t),
        grid=(b, ATT_KV_HEADS, t // tq),
        in_specs=in_specs,
        out_specs=pl.BlockSpec((1, tq, gw), lambda bi, h, n: (bi, n, h)),
        out_shape=jax.ShapeDtypeStruct((b, t, ATT_W), bf16),
        compiler_params=_cparams(("arbitrary", "arbitrary", "arbitrary")),
        name="attn_band" if band else "attn_ctx",
    )(*args)


def _ssd_kernel(xc_ref, bc_ref, cc_ref, dtc_ref, zc_ref, xl_ref, bl_ref, cl_ref, dtl_ref, zl_ref,
                cw_ref, cb_ref, dtb_ref, a_ref, acol_ref, dsk_ref, ng_ref, oc_ref, ol_ref,
                pad_scr, xs_scr, cm_scr, bt_scr, dts_scr, dtt_scr, y_scr, ydir_scr, h_scr, *, lc, seq):
    ck = SSD_CHUNK
    gw = SSD_GW
    nst = SSD_STATE
    row = lax.broadcasted_iota(jnp.int32, (ck, ck), 0)
    col = lax.broadcasted_iota(jnp.int32, (ck, ck), 1)
    tri = (row >= col, col >= row)
    tri_bf = tuple(jnp.where(t, 1.0, 0.0).astype(bf16) for t in tri)
    first = col < SSD_HEAD_DIM
    h_scr[...] = jnp.zeros(h_scr.shape, f32)

    def run_seq(t, x_ref, b_ref, c_ref, dt_ref, z_ref, o_ref):
        nc = t // ck
        zero8 = jnp.zeros((8, gw + 2 * nst), f32)
        pad_scr[0:8, :] = zero8
        pad_scr[8 + t:16 + t, :] = zero8

        def fill(c, carry):
            r0 = pl.multiple_of(c * ck, ck)
            pad_scr[pl.ds(r0 + 8, ck), 0:gw] = x_ref[0, pl.ds(r0, ck), :].astype(f32)
            pad_scr[pl.ds(r0 + 8, ck), gw:gw + nst] = b_ref[0, pl.ds(r0, ck), :].astype(f32)
            pad_scr[pl.ds(r0 + 8, ck), gw + nst:gw + 2 * nst] = c_ref[0, pl.ds(r0, ck), :].astype(f32)
            return carry

        lax.fori_loop(0, nc, fill, 0)

        def conv(c, carry):
            r0 = pl.multiple_of(c * ck, ck)
            xm, x0, xp = _shifted_rows(pad_scr, r0, ck)
            act = _silu(cw_ref[0, 0:1, :] * xm + cw_ref[0, 1:2, :] * x0 + cw_ref[0, 2:3, :] * xp + cb_ref[0])
            xs = act[:, 0:gw]
            y_scr[pl.ds(r0, ck), :] = dsk_ref[...] * xs
            xs_scr[pl.ds(r0, ck), :] = xs.astype(bf16)
            bt_scr[c] = act[:, gw:gw + nst].T
            cm_scr[pl.ds(r0, ck), :] = act[:, gw + nst:gw + 2 * nst].astype(bf16)
            dv = dt_ref[0, pl.ds(r0, ck), :] + dtb_ref[0]
            dts = jnp.maximum(dv, 0.0) + jnp.log1p(jnp.exp(-jnp.abs(dv)))
            dts_scr[pl.ds(r0, ck), :] = dts
            dtt_scr[c] = dts.T[0:DT_ROWS, :]
            return carry

        lax.fori_loop(0, nc, conv, 0)

        def chunk(i, carry):
            dirs = (0, 1)
            streams = [(t, d) for t in range(per) for d in dirs]
            cidx = {(t, d): (i * per + t if d == 0 else nc - 1 - (i * per + t)) for (t, d) in streams}
            r0 = {s: pl.multiple_of(cidx[s] * ck, ck) for s in streams}
            cb16 = {s: cm_scr[pl.ds(r0[s], ck), :] for s in streams}
            bt32 = {s: bt_scr[cidx[s]] for s in streams}
            dt = {s: dts_scr[pl.ds(r0[s], ck), :] for s in streams}
            dt_t = {s: dtt_scr[cidx[s]] for s in streams}
            sp = {s: _split3(dt[s] * a_ref[0]) for s in streams}
            sp_t = {s: _split3(dt_t[s] * acol_ref[0, 0]) for s in streams}
            c3 = {s: jnp.dot(tri_bf[s[1]], jnp.concatenate(sp[s], axis=1), preferred_element_type=f32)
                  for s in streams}
            cum = {s: c3[s][:, 0:LANES] + c3[s][:, LANES:2 * LANES] + c3[s][:, 2 * LANES:] for s in streams}
            c3t = {s: jnp.dot(jnp.concatenate(sp_t[s], axis=0), tri_bf[1 - s[1]], preferred_element_type=f32)
                   for s in streams}
            cum_t = {s: c3t[s][0:DT_ROWS] + c3t[s][DT_ROWS:2 * DT_ROWS] + c3t[s][2 * DT_ROWS:] for s in streams}
            cbm = {s: jnp.dot(cb16[s], bt32[s].astype(bf16), preferred_element_type=f32) for s in streams}
            tot = {s: (cum[s][ck - 1:ck, :] if s[1] == 0 else cum[s][0:1, :]) for s in streams}
            tot_t = {s: (cum_t[s][:, ck - 1:ck] if s[1] == 0 else cum_t[s][:, 0:1]) for s in streams}
            w_t = {s: dt_t[s] * jnp.exp(tot_t[s] - cum_t[s]) for s in streams}
            etot = {s: jnp.exp(tot[s]) for s in streams}
            pairs = range(SSD_KH // 2)
            units = [(t, p, d) for t in range(per) for p in pairs for d in dirs]
            xp16 = {(t, p, d): xs_scr[pl.ds(r0[t, d], ck), p * LANES:(p + 1) * LANES] for (t, p, d) in units}
            mks, ccols, bws = {}, {}, {}
            for kk in range(2):
                for (t, p, d) in units:
                    s = (t, d)
                    j = d * SSD_KH + 2 * p + kk
                    ccols[t, p, d, kk] = jnp.broadcast_to(cum[s][:, j:j + 1], (ck, ck))
                    decay = jnp.exp(jnp.where(tri[d], ccols[t, p, d, kk] - cum_t[s][j:j + 1, :], NEG_INF))
                    mks[t, p, d, kk] = (cbm[s] * decay * dt_t[s][j:j + 1, :]).astype(bf16)
                    bws[t, p, d, kk] = (bt32[s] * w_t[s][j:j + 1, :]).astype(bf16)
            yy, ss = {}, {}
            for u in units:
                yy[u] = jnp.dot(jnp.concatenate([mks[u + (0,)], mks[u + (1,)]], axis=0), xp16[u],
                                preferred_element_type=f32)
                ss[u] = jnp.dot(jnp.concatenate([bws[u + (0,)], bws[u + (1,)]], axis=0), xp16[u],
                                preferred_element_type=f32)
            hcur = {(p, d): h_scr[d * (SSD_KH // 2) + p] for p in pairs for d in dirs}
            for (t, p, d) in units:
                s, u = (t, d), (t, p, d)
                j0 = d * SSD_KH + 2 * p
                ch = jnp.dot(cb16[s], hcur[p, d].astype(bf16), preferred_element_type=f32)
                ydiag = jnp.where(first, yy[u][0:ck], yy[u][ck:])
                yoff = jnp.exp(jnp.where(first, ccols[u + (0,)], ccols[u + (1,)])) * ch
                ydir_scr[d, pl.ds(r0[s], ck), p * LANES:(p + 1) * LANES] = ydiag + yoff
                erow = jnp.where(first[0:1, :], etot[s][:, j0:j0 + 1], etot[s][:, j0 + 1:j0 + 2])
                hcur[p, d] = hcur[p, d] * erow + jnp.where(first, ss[u][0:nst], ss[u][nst:])
            for p in pairs:
                for d in dirs:
                    h_scr[d * (SSD_KH // 2) + p] = hcur[p, d]
            return carry

        per = SSD_TRIP_CHUNKS if nc % SSD_TRIP_CHUNKS == 0 else 1
        lax.fori_loop(0, nc // per, chunk, 0)

        def gate(c, carry):
            r0 = pl.multiple_of(c * ck, ck)
            y = y_scr[pl.ds(r0, ck), :] + ydir_scr[0, pl.ds(r0, ck), :] + ydir_scr[1, pl.ds(r0, ck), :]
            u = y * _silu(z_ref[0, pl.ds(r0, ck), :].astype(f32))
            u = u * lax.rsqrt(jnp.mean(u * u, axis=-1, keepdims=True) + NORM_EPS)
            o_ref[0, pl.ds(r0, ck), :] = (u * ng_ref[...]).astype(bf16)
            return carry

        lax.fori_loop(0, nc, gate, 0)

    run_seq(lc, xc_ref, bc_ref, cc_ref, dtc_ref, zc_ref, oc_ref)
    run_seq(seq, xl_ref, bl_ref, cl_ref, dtl_ref, zl_ref, ol_ref)


def _ssd(xbc_c, dt_c, z_c, xbc_l, dt_l, z_l, cw, cb, dtb, a_neg, a_col, dsk, ng):
    b, lc, _ = xbc_c.shape
    seq = xbc_l.shape[1]
    gw, nst = SSD_GW, SSD_STATE
    xoff = SSD_W // nst

    def seq_specs(t):
        return [pl.BlockSpec((1, t, gw), lambda bi, g: (bi, 0, g)),
                pl.BlockSpec((1, t, nst), lambda bi, g: (bi, 0, xoff + g)),
                pl.BlockSpec((1, t, nst), lambda bi, g: (bi, 0, xoff + SSD_GROUPS + g)),
                pl.BlockSpec((1, t, LANES), lambda bi, g: (bi, 0, g)),
                pl.BlockSpec((1, t, gw), lambda bi, g: (bi, 0, g))]

    cwid = gw + 2 * nst
    par_specs = [pl.BlockSpec((1, 3, cwid), lambda bi, g: (g, 0, 0)),
                 pl.BlockSpec((1, 1, cwid), lambda bi, g: (g, 0, 0)),
                 pl.BlockSpec((1, 1, LANES), lambda bi, g: (g, 0, 0)),
                 pl.BlockSpec((1, 1, LANES), lambda bi, g: (g, 0, 0)),
                 pl.BlockSpec((1, 1, DT_ROWS, 1), lambda bi, g: (g, 0, 0, 0)),
                 pl.BlockSpec((1, gw), lambda bi, g: (0, g)),
                 pl.BlockSpec((1, gw), lambda bi, g: (0, g))]
    nck = seq // SSD_CHUNK
    return pl.pallas_call(
        functools.partial(_ssd_kernel, lc=lc, seq=seq),
        grid=(b, SSD_GROUPS),
        in_specs=seq_specs(lc) + seq_specs(seq) + par_specs,
        out_specs=[pl.BlockSpec((1, lc, gw), lambda bi, g: (bi, 0, g)),
                   pl.BlockSpec((1, seq, gw), lambda bi, g: (bi, 0, g))],
        out_shape=[jax.ShapeDtypeStruct((b, lc, SSD_W), bf16), jax.ShapeDtypeStruct((b, seq, SSD_W), bf16)],
        scratch_shapes=[pltpu.VMEM((seq + 16, cwid), f32),
                        pltpu.VMEM((seq, gw), bf16),
                        pltpu.VMEM((seq, nst), bf16),
                        pltpu.VMEM((nck, nst, SSD_CHUNK), f32),
                        pltpu.VMEM((seq, LANES), f32),
                        pltpu.VMEM((nck, DT_ROWS, SSD_CHUNK), f32),
                        pltpu.VMEM((seq, gw), f32),
                        pltpu.VMEM((2, seq, gw), f32),
                        pltpu.VMEM((SSD_KH, nst, LANES), f32)],
        compiler_params=_cparams(("arbitrary", "arbitrary")),
        name="ssd",
    )(xbc_c, xbc_c, xbc_c, dt_c, z_c, xbc_l, xbc_l, xbc_l, dt_l, z_l, cw, cb, dtb, a_neg, a_col, dsk, ng)


def _sconv_kernel(b_ref, c_ref, h_ref, w_ref, o_ref, pad_scr, *, t):
    ck = min(256, t)
    zero8 = jnp.zeros((8, LANES), f32)
    pad_scr[0:8, :] = zero8
    pad_scr[8 + t:16 + t, :] = zero8

    def fill(c, carry):
        r0 = pl.multiple_of(c * ck, ck)
        pad_scr[pl.ds(r0 + 8, ck), :] = c_ref[0, pl.ds(r0, ck), :].astype(f32) * h_ref[0, pl.ds(r0, ck), :].astype(f32)
        return carry

    lax.fori_loop(0, t // ck, fill, 0)

    def conv(c, carry):
        r0 = pl.multiple_of(c * ck, ck)
        xm, x0, xp = _shifted_rows(pad_scr, r0, ck)
        acc = w_ref[0:1, :] * xm + w_ref[1:2, :] * x0 + w_ref[2:3, :] * xp
        o_ref[0, pl.ds(r0, ck), :] = (b_ref[0, pl.ds(r0, ck), :].astype(f32) * acc).astype(bf16)
        return carry

    lax.fori_loop(0, t // ck, conv, 0)


def _sconv(scb, scc, sch, w):
    b, t, cw = scb.shape
    spec = pl.BlockSpec((1, t, LANES), lambda bi, j: (bi, 0, j))
    return pl.pallas_call(
        functools.partial(_sconv_kernel, t=t),
        grid=(b, cw // LANES),
        in_specs=[spec, spec, spec, pl.BlockSpec((3, LANES), lambda bi, j: (0, j))],
        out_specs=spec,
        out_shape=jax.ShapeDtypeStruct((b, t, cw), bf16),
        scratch_shapes=[pltpu.VMEM((t + 16, LANES), f32)],
        compiler_params=_cparams(("arbitrary", "arbitrary")),
        name="sconv",
    )(scb, scc, sch, w)


def _outproj_kernel(att_ref, ssd_ref, sc_ref, x_ref, g1_ref, sh2_ref, sc2_ref, n2_ref, wo_ref, wr_ref,
                    xo_ref, h2_ref, lg_ref, *, sub):
    for r0 in range(0, x_ref.shape[0], sub):
        rs = slice(r0, r0 + sub)
        acc = jnp.dot(att_ref[rs, :], wo_ref[0, 0:ATT_W, :], preferred_element_type=f32)
        acc = acc + jnp.dot(ssd_ref[rs, :], wo_ref[0, ATT_W:ATT_W + SSD_W, :], preferred_element_type=f32)
        acc = acc + jnp.dot(sc_ref[rs, :], wo_ref[0, ATT_W + SSD_W:, :], preferred_element_type=f32)
        x = x_ref[rs, :] + g1_ref[0] * acc
        xo_ref[rs, :] = x
        y = x * lax.rsqrt(jnp.mean(x * x, axis=-1, keepdims=True) + NORM_EPS) * n2_ref[...]
        h2 = y * (1.0 + sc2_ref[0]) + sh2_ref[0]
        hh = h2.astype(bf16)
        h2_ref[rs, :] = hh
        hm = (h2 - hh.astype(f32)).astype(bf16)
        r1 = jnp.dot(hh, wr_ref[0], preferred_element_type=f32)
        r2 = jnp.dot(hm, wr_ref[0, :, 0:ROUTER_PAD], preferred_element_type=f32)
        lg_ref[rs, :] = r1[:, 0:ROUTER_PAD] + r1[:, ROUTER_PAD:] + r2


def _outproj(att, ssd, sconv, x2d, g1, sh2, sc2, n2, wo, wr2, layer, tm, rows_per_mod):
    m, d = x2d.shape
    tpm = rows_per_mod // tm
    mod_spec = pl.BlockSpec((1, 1, d), lambda i: (i // tpm, 0, 0))
    return pl.pallas_call(
        functools.partial(_outproj_kernel, sub=min(256, tm)),
        grid=(m // tm,),
        in_specs=[pl.BlockSpec((tm, ATT_W), lambda i: (i, 0)), pl.BlockSpec((tm, SSD_W), lambda i: (i, 0)),
                  pl.BlockSpec((tm, SC_W), lambda i: (i, 0)), pl.BlockSpec((tm, d), lambda i: (i, 0)),
                  mod_spec, mod_spec, mod_spec, pl.BlockSpec((1, d), lambda i: (0, 0)),
                  pl.BlockSpec((1, d, d), lambda i: (layer, 0, 0)),
                  pl.BlockSpec((1, d, 2 * ROUTER_PAD), lambda i: (layer, 0, 0))],
        out_specs=[pl.BlockSpec((tm, d), lambda i: (i, 0)), pl.BlockSpec((tm, d), lambda i: (i, 0)),
                   pl.BlockSpec((tm, ROUTER_PAD), lambda i: (i, 0))],
        out_shape=[jax.ShapeDtypeStruct((m, d), f32), jax.ShapeDtypeStruct((m, d), bf16),
                   jax.ShapeDtypeStruct((m, ROUTER_PAD), f32)],
        compiler_params=_cparams(("arbitrary",)),
        name="outproj",
    )(att, ssd, sconv, x2d, g1, sh2, sc2, n2, wo, wr2)


def _ffn_kernel(*refs, nt, with_ctx):
    if with_ctx:
        x_ref, gv_ref, xc_ref, gvc_ref, wg_ref, wu_ref, wd_ref, o_ref, oc_ref = refs
    else:
        x_ref, gv_ref, wg_ref, wu_ref, wd_ref, o_ref = refs

    def ffn(xr, gr, outr):
        x = xr[0]
        a = jnp.dot(x, wg_ref[0, 0].astype(bf16), preferred_element_type=f32)
        u = jnp.dot(x, wu_ref[0, 0].astype(bf16), preferred_element_type=f32)
        hmid = (_silu(a) * u).astype(bf16)
        outr[0] = (jnp.dot(hmid, wd_ref[0, 0].astype(bf16), preferred_element_type=f32) * gr[0]).astype(bf16)

    if not with_ctx:
        ffn(x_ref, gv_ref, o_ref)
        return
    i = pl.program_id(1)

    @pl.when(i < nt)
    def _():
        ffn(x_ref, gv_ref, o_ref)

    @pl.when(i == nt)
    def _():
        ffn(xc_ref, gvc_ref, oc_ref)


def _ffn_tile(nrows):
    for nt in range(1, nrows + 1):
        if nrows % nt == 0 and (nrows // nt) % 16 == 0 and nrows // nt <= FFN_MAX_TILE:
            return nrows // nt
    raise ValueError(f"no aligned row tile for {nrows} gathered rows")


def _expert_ffn(xg, gv, wg, wu, wd, layer, xgc=None, gvc=None):
    e, r, d = xg.shape
    ff = wg.shape[3]
    tm = _ffn_tile(r)
    nt = r // tm
    with_ctx = xgc is not None
    last = nt - 1
    row_specs = [pl.BlockSpec((1, tm, d), lambda ei, i: (ei, jnp.minimum(i, last), 0)),
                 pl.BlockSpec((1, tm, 1), lambda ei, i: (ei, jnp.minimum(i, last), 0))]
    one = pl.Buffered(1)
    w_specs = [pl.BlockSpec((1, 1, d, ff), lambda ei, i: (layer, ei, 0, 0), pipeline_mode=one),
               pl.BlockSpec((1, 1, d, ff), lambda ei, i: (layer, ei, 0, 0)),
               pl.BlockSpec((1, 1, ff, d), lambda ei, i: (layer, ei, 0, 0))]
    out_specs = [pl.BlockSpec((1, tm, d), lambda ei, i: (ei, jnp.minimum(i, last), 0))]
    out_shape = [jax.ShapeDtypeStruct((e, r, d), bf16)]
    args = [xg, gv]
    if with_ctx:
        rc = xgc.shape[1]
        row_specs += [pl.BlockSpec((1, rc, d), lambda ei, i: (ei, 0, 0)),
                      pl.BlockSpec((1, rc, 1), lambda ei, i: (ei, 0, 0))]
        out_specs.append(pl.BlockSpec((1, rc, d), lambda ei, i: (ei, 0, 0)))
        out_shape.append(jax.ShapeDtypeStruct((e, rc, d), bf16))
        args += [xgc, gvc]
    outs = pl.pallas_call(
        functools.partial(_ffn_kernel, nt=nt, with_ctx=with_ctx),
        grid=(e, nt + (1 if with_ctx else 0)),
        in_specs=row_specs + w_specs,
        out_specs=out_specs,
        out_shape=out_shape,
        compiler_params=_cparams(("arbitrary", "arbitrary")),
        name="expert_ffn",
    )(*args, wg, wu, wd)
    return outs if with_ctx else outs[0]


def _combine_kernel(hit_ref, idx_ref, y_ref, xm_ref, g2_ref, o_ref, acc_scr, *, sb):
    bi = pl.program_id(0)
    k = pl.program_id(1)
    e = pl.program_id(2)
    nb, nk, ne = pl.num_programs(0), pl.num_programs(1), pl.num_programs(2)
    tt = acc_scr.shape[0]
    nsb = y_ref.shape[1] // sb

    @pl.when(e == 0)
    def _():
        acc_scr[...] = jnp.zeros(acc_scr.shape, f32)

    tok = lax.broadcasted_iota(jnp.int32, (tt, sb), 0) + k * tt
    for j in range(nsb):
        @pl.when(hit_ref[((e * nb + bi) * nk + k) * nsb + j] != 0)
        def _():
            onehot = jnp.where(tok == idx_ref[0, 0, :, j * sb:(j + 1) * sb], 1.0, 0.0).astype(bf16)
            acc_scr[...] += jnp.dot(onehot, y_ref[0, j * sb:(j + 1) * sb, :], preferred_element_type=f32)

    @pl.when(e == ne - 1)
    def _():
        o_ref[...] = xm_ref[...] + g2_ref[0] * acc_scr[...]


def _combine(idx, y, xm, g2, b, t, per_sample_gate):
    e, _, _, cap = idx.shape
    d = xm.shape[1]
    tt = min(COMBINE_TILE, t)
    nk = t // tt
    sb = min(COMBINE_SLOTS, cap)
    nsb = cap // sb
    blk = idx.reshape(e, b, nsb, sb)
    lo = blk[..., 0][:, :, None, :]
    hi = blk[..., sb - 1][:, :, None, :]
    tile0 = (jnp.arange(nk, dtype=idx.dtype) * tt)[None, None, :, None]
    hit = ((lo < tile0 + tt) & (hi >= tile0)).astype(jnp.int32).reshape(-1)
    grid_spec = pltpu.PrefetchScalarGridSpec(
        num_scalar_prefetch=1,
        grid=(b, nk, e),
        in_specs=[pl.BlockSpec((1, 1, 1, cap), lambda bi, k, ei, hit_ref: (ei, bi, 0, 0)),
                  pl.BlockSpec((1, cap, d), lambda bi, k, ei, hit_ref: (ei, bi, 0)),
                  pl.BlockSpec((tt, d), lambda bi, k, ei, hit_ref: (bi * nk + k, 0)),
                  pl.BlockSpec((1, 1, d), lambda bi, k, ei, hit_ref: (bi if per_sample_gate else 0, 0, 0))],
        out_specs=pl.BlockSpec((tt, d), lambda bi, k, ei, hit_ref: (bi * nk + k, 0)),
        scratch_shapes=[pltpu.VMEM((tt, d), f32)])
    return pl.pallas_call(
        functools.partial(_combine_kernel, sb=sb),
        grid_spec=grid_spec,
        out_shape=jax.ShapeDtypeStruct(xm.shape, f32),
        compiler_params=_cparams(("arbitrary", "arbitrary", "arbitrary")),
        name="moe_combine",
    )(hit, idx, y, xm, g2)


def _rope_tables(n_tokens):
    rows = n_tokens // GRID_W
    row = jnp.repeat(jnp.arange(rows), GRID_W).astype(f32)
    col = jnp.tile(jnp.arange(GRID_W), rows).astype(f32)
    n_freq = HEAD_DIM // 4
    inv = ROPE_THETA ** (-jnp.arange(n_freq, dtype=f32) / n_freq)
    ang_r = row[:, None] * inv
    ang_c = col[:, None] * inv
    cos = jnp.concatenate([jnp.cos(ang_r), jnp.cos(ang_r), jnp.cos(ang_c), jnp.cos(ang_c)], axis=-1)
    sins = jnp.concatenate([-jnp.sin(ang_r), jnp.sin(ang_r), -jnp.sin(ang_c), jnp.sin(ang_c)], axis=-1)
    return cos, sins


def _dt_layout(v):
    lead = v.shape[:-1]
    v = v.reshape(lead + (2, SSD_GROUPS, SSD_KH))
    v = jnp.moveaxis(v, -2, -3).reshape(lead + (SSD_GROUPS, 2 * SSD_KH))
    v = jnp.pad(v, [(0, 0)] * (len(lead) + 1) + [(0, LANES - 2 * SSD_KH)])
    return v.reshape(lead + (DT_PAD,))


def _group_layout(v):
    outs = []
    for g in range(SSD_GROUPS):
        outs.append(jnp.concatenate([
            v[..., g * SSD_GW:(g + 1) * SSD_GW],
            v[..., SSD_W + g * SSD_STATE:SSD_W + (g + 1) * SSD_STATE],
            v[..., SSD_W + SSD_BC_W + g * SSD_STATE:SSD_W + SSD_BC_W + (g + 1) * SSD_STATE]], axis=-1))
    return jnp.stack(outs)


def _route(logits, b, t):
    cap = EC_FACTOR * t // N_EXPERTS
    aff = jax.nn.softmax(logits[:, :N_EXPERTS].reshape(b, t, N_EXPERTS), axis=-1)
    gval, idx = lax.top_k(jnp.swapaxes(aff, 1, 2), cap)
    idx, gval = lax.sort((idx, gval), dimension=-1, num_keys=1)
    rows = idx + (jnp.arange(b, dtype=idx.dtype) * t)[:, None, None]
    to_e = lambda u: jnp.swapaxes(u, 0, 1).reshape(N_EXPERTS, b * cap)
    return to_e(gval)[..., None], to_e(rows), jnp.swapaxes(idx, 0, 1).reshape(N_EXPERTS, b, 1, cap)


def kernel(x, c, ctx, c_ctx, norm1_g, norm2_g, w_mod, b_mod, w_in, q_norm_g, k_norm_g, attn_sink, ssd_conv_w,
           ssd_conv_b, ssd_dt_bias, ssd_a_log, ssd_d, ssd_norm_g, sc_conv_w, w_out, w_router, w_expert_gate,
           w_expert_up, w_expert_down):
    b, s, d = x.shape
    lc = ctx.shape[1]
    nl = w_in.shape[0]
    cos, sins = _rope_tables(s)

    nrow = -(-(b + 1) // 8) * 8
    cvec = jnp.zeros((nrow, d), f32).at[:b].set(c).at[b].set(c_ctx)
    mods = _modulations(cvec, w_mod, b_mod).reshape(nl, nrow, N_MOD, 1, d)

    cuts = [0]
    for wdt in (ATT_W, ATT_KV_W, ATT_KV_W, SSD_W, SSD_XBC_W, SSD_DT_W, SC_W, SC_W, SC_W):
        cuts.append(cuts[-1] + wdt)
    w_dt = _dt_layout(w_in[:, :, cuts[5]:cuts[6]])
    w_in_p = jnp.concatenate([w_in[:, :, :cuts[5]], w_dt, w_in[:, :, cuts[6]:]], axis=-1).astype(bf16)
    w_out_b = w_out.astype(bf16)
    wr_pad = jnp.pad(w_router, ((0, 0), (0, 0), (0, ROUTER_PAD - N_EXPERTS)))
    wr_hi, wr_mid, _ = _split3(wr_pad)
    wr2 = jnp.concatenate([wr_hi, wr_mid], axis=-1)
    dtb = _dt_layout(ssd_dt_bias.reshape(nl, SSD_DT_W)).reshape(nl, SSD_GROUPS, 1, LANES)
    a_neg = _dt_layout(-jnp.exp(ssd_a_log.reshape(nl, SSD_DT_W))).reshape(nl, SSD_GROUPS, 1, LANES)
    a_col = a_neg[..., :DT_ROWS].reshape(nl, SSD_GROUPS, 1, DT_ROWS, 1)
    dsk = jnp.repeat(ssd_d, SSD_HEAD_DIM, axis=-1).reshape(nl, 1, SSD_W)

    xl = x.reshape(b * s, d)
    xc = ctx.reshape(b * lc, d)
    for i in range(nl):
        with_ctx_out = i < nl - 1
        sh1, sc1, g1, sh2, sc2, g2 = (mods[i, :, j] for j in range(N_MOD))
        n1 = norm1_g[i].reshape(1, d)
        n2 = norm2_g[i].reshape(1, d)
        qg = q_norm_g[i].reshape(1, HEAD_DIM)
        kg = k_norm_g[i].reshape(1, HEAD_DIM)

        q, k, v, z, xbc, dtr, scb, scc, sch = _inproj(xl, sh1, sc1, n1, w_in_p, i, 512, s)
        qc, kc, vc, zc, xbcc, dtrc, scbc, sccc, schc = _inproj(xc, sh1[b:], sc1[b:], n1, w_in_p, i, lc, b * lc)

        kp = _kprep(k, kg, cos, sins, s, True)
        kcp = _kprep(kc, kg, cos, sins, lc, False)
        r3 = lambda u, t: u.reshape(b, t, u.shape[-1])
        att = _attention(r3(q, s), qg, attn_sink[i], r3(kcp, lc), r3(vc, lc), r3(kp, s), r3(v, s), cos, sins)

        ssd_c, ssd_l = _ssd(r3(xbcc, lc), r3(dtrc, lc), r3(zc, lc), r3(xbc, s), r3(dtr, s), r3(z, s),
                            _group_layout(ssd_conv_w[i]), _group_layout(ssd_conv_b[i].reshape(1, -1)),
                            dtb[i], a_neg[i], a_col[i], dsk[i], ssd_norm_g[i].reshape(1, SSD_W))
        sconv = _sconv(r3(scb, s), r3(scc, s), r3(sch, s), sc_conv_w[i])

        xm, h2, lg = _outproj(att.reshape(b * s, ATT_W), ssd_l.reshape(b * s, SSD_W), sconv.reshape(b * s, SC_W),
                              xl, g1, sh2, sc2, n2, w_out_b, wr2, i, 512, s)
        gv, rows, idx = _route(lg, b, s)
        if with_ctx_out:
            attc = _attention(r3(qc, lc), qg, attn_sink[i], r3(kcp, lc), r3(vc, lc))
            sconvc = _sconv(r3(scbc, lc), r3(sccc, lc), r3(schc, lc), sc_conv_w[i])
            xmc, h2c, lgc = _outproj(attc.reshape(b * lc, ATT_W), ssd_c.reshape(b * lc, SSD_W),
                                     sconvc.reshape(b * lc, SC_W), xc, g1[b:], sh2[b:], sc2[b:], n2,
                                     w_out_b, wr2, i, lc, b * lc)
            gvc, rowsc, idxc = _route(lgc, b, lc)
            y, yc = _expert_ffn(h2[rows], gv, w_expert_gate, w_expert_up, w_expert_down, i, h2c[rowsc], gvc)
            xc = _combine(idxc, yc, xmc, g2[b:], b, lc, False)
        else:
            y = _expert_ffn(h2[rows], gv, w_expert_gate, w_expert_up, w_expert_down, i)
        xl = _combine(idx, y, xm, g2, b, s, True)
    return xl.reshape(b, s, d)
```

```python
import functools

import jax
import jax.numpy as jnp
from jax import lax
from jax.experimental import pallas as pl
from jax.experimental.pallas import tpu as pltpu

f32 = jnp.float32
bf16 = jnp.bfloat16

D_MODEL = 2048
DEPTH = 4
GRID_W = 64
NORM_EPS = 1e-6
N_MOD = 6
HEAD_DIM = 128
ATT_HEADS = 8
ATT_KV_HEADS = 2
ATT_GROUP = ATT_HEADS // ATT_KV_HEADS
ATT_WINDOW = 128
ATT_BLOCK = 128
ROPE_THETA = 10000.0
SSD_HEAD_DIM = 64
SSD_W = 512
SSD_HEADS = 8
SSD_GROUPS = 2
SSD_KH = SSD_HEADS // SSD_GROUPS
SSD_STATE = 128
SSD_CHUNK = 128
SSD_GW = SSD_W // SSD_GROUPS
SC_W = 512
ATT_W = ATT_HEADS * HEAD_DIM
ATT_KV_W = ATT_KV_HEADS * HEAD_DIM
SSD_BC_W = SSD_GROUPS * SSD_STATE
SSD_XBC_W = SSD_W + 2 * SSD_BC_W
SSD_DT_W = 2 * SSD_HEADS
N_EXPERTS = 16
EC_FACTOR = 2
EXPERT_FF = 1024

LANES = 128
DT_PAD = SSD_GROUPS * LANES
ROUTER_PAD = LANES
SEG_W = (ATT_W, ATT_KV_W, ATT_KV_W, SSD_W, SSD_XBC_W, DT_PAD, SC_W, SC_W, SC_W)
SEG_DT = (bf16, bf16, bf16, bf16, bf16, f32, bf16, bf16, bf16)
IN_W_PAD = sum(SEG_W)
VMEM_LIMIT = 58 * 1024 * 1024
NEG_INF = float("-inf")
LOG2E = 1.4426950408889634
ATT_TQ = 1024
ATT_GROUP_BLOCKS = 8
COMBINE_TILE = 1024
COMBINE_SLOTS = 256
DT_ROWS = 16
SSD_TRIP_CHUNKS = 2
KPREP_TILE = 2048
FFN_MAX_TILE = 576


def _cparams(sem):
    return pltpu.CompilerParams(dimension_semantics=sem, vmem_limit_bytes=VMEM_LIMIT)


def _silu(v):
    return v * jax.nn.sigmoid(v)


def _shifted_rows(pad_ref, r0, n):
    x0 = pad_ref[pl.ds(r0 + 8, n), :]
    before = pad_ref[pl.ds(r0, 8), :][7:8, :]
    after = pad_ref[pl.ds(r0 + 8 + n, 8), :][0:1, :]
    ri = lax.broadcasted_iota(jnp.int32, x0.shape, 0)
    xm = jnp.where(ri == 0, before, pltpu.roll(x0, 1, 0))
    xp = jnp.where(ri == n - 1, after, pltpu.roll(x0, n - 1, 0))
    return xm, x0, xp


def _split3(v):
    hi = v.astype(bf16)
    r = v - hi.astype(f32)
    mid = r.astype(bf16)
    lo = (r - mid.astype(f32)).astype(bf16)
    return hi, mid, lo


def _mod_kernel(c_ref, w_ref, b_ref, o_ref):
    a = _silu(c_ref[...]).astype(bf16)
    o_ref[0] = jnp.dot(a, w_ref[0].astype(bf16), preferred_element_type=f32) + b_ref[0]


def _modulations(cvec, w_mod, b_mod):
    nl, d, n = w_mod.shape
    r = cvec.shape[0]
    tn = 1024
    return pl.pallas_call(
        _mod_kernel,
        grid=(nl, n // tn),
        in_specs=[pl.BlockSpec((r, d), lambda l, j: (0, 0)),
                  pl.BlockSpec((1, d, tn), lambda l, j: (l, 0, j)),
                  pl.BlockSpec((1, 1, tn), lambda l, j: (l, 0, j))],
        out_specs=pl.BlockSpec((1, r, tn), lambda l, j: (l, 0, j)),
        out_shape=jax.ShapeDtypeStruct((nl, r, n), f32),
        compiler_params=_cparams(("arbitrary", "arbitrary")),
        name="modulations",
    )(cvec, w_mod, b_mod.reshape(nl, 1, n))


def _inproj_kernel(x_ref, sh_ref, sc_ref, g_ref, w_ref, *rest):
    outs, h_scr = rest[:-1], rest[-1]
    x = x_ref[...]
    y = x * lax.rsqrt(jnp.mean(x * x, axis=-1, keepdims=True) + NORM_EPS) * g_ref[...]
    h_scr[...] = (y * (1.0 + sc_ref[0]) + sh_ref[0]).astype(bf16)
    off = 0
    for ref, width in zip(outs, SEG_W):
        for c0 in range(0, width, 512):
            cw = min(512, width - c0)
            ref[:, c0:c0 + cw] = jnp.dot(h_scr[...], w_ref[0, :, off + c0:off + c0 + cw],
                                         preferred_element_type=f32).astype(ref.dtype)
        off += width


def _inproj(x2d, shift, scale, g, w, layer, tm, rows_per_mod):
    m, d = x2d.shape
    tpm = rows_per_mod // tm
    mod_spec = pl.BlockSpec((1, 1, d), lambda i: (i // tpm, 0, 0))
    return pl.pallas_call(
        _inproj_kernel,
        grid=(m // tm,),
        in_specs=[pl.BlockSpec((tm, d), lambda i: (i, 0)), mod_spec, mod_spec,
                  pl.BlockSpec((1, d), lambda i: (0, 0)),
                  pl.BlockSpec((1, d, IN_W_PAD), lambda i: (layer, 0, 0), pipeline_mode=pl.Buffered(1))],
        out_specs=[pl.BlockSpec((tm, wd), lambda i: (i, 0)) for wd in SEG_W],
        out_shape=[jax.ShapeDtypeStruct((m, wd), dt) for wd, dt in zip(SEG_W, SEG_DT)],
        scratch_shapes=[pltpu.VMEM((tm, d), bf16)],
        compiler_params=_cparams(("arbitrary",)),
        name="inproj",
    )(x2d, shift, scale, g, w)


def _norm_rope(v, g, cos, sins):
    y = v * lax.rsqrt(jnp.mean(v * v, axis=-1, keepdims=True) + NORM_EPS) * g
    if cos is None:
        return y
    lane = lax.broadcasted_iota(jnp.int32, y.shape, 1)
    quarter = HEAD_DIM // 4
    partner = jnp.where((lane % (2 * quarter)) < quarter,
                        pltpu.roll(y, HEAD_DIM - quarter, 1), pltpu.roll(y, quarter, 1))
    return y * cos + partner * sins


def _kprep_kernel(k_ref, g_ref, cos_ref, sin_ref, o_ref, *, rope):
    for hh in range(ATT_KV_HEADS):
        sl = slice(hh * HEAD_DIM, (hh + 1) * HEAD_DIM)
        v = k_ref[:, sl].astype(f32)
        o_ref[:, sl] = _norm_rope(v, g_ref[...], cos_ref[...] if rope else None,
                                  sin_ref[...] if rope else None).astype(bf16)


def _kprep(k2d, g, cos, sins, seq, rope):
    m = k2d.shape[0]
    tk = min(KPREP_TILE, seq)
    nt = seq // tk
    tab = pl.BlockSpec((tk, HEAD_DIM), lambda i: (i % nt, 0))
    return pl.pallas_call(
        functools.partial(_kprep_kernel, rope=rope),
        grid=(m // tk,),
        in_specs=[pl.BlockSpec((tk, ATT_KV_W), lambda i: (i, 0)),
                  pl.BlockSpec((1, HEAD_DIM), lambda i: (0, 0)), tab, tab],
        out_specs=pl.BlockSpec((tk, ATT_KV_W), lambda i: (i, 0)),
        out_shape=jax.ShapeDtypeStruct((m, ATT_KV_W), bf16),
        compiler_params=_cparams(("arbitrary",)),
        name="kprep_rope" if rope else "kprep",
    )(k2d, g, cos, sins)


def _attn_kernel(*refs, band, tq, seq):
    if band:
        (sink_ref, q_ref, cos_ref, sin_ref, qg_ref, kp_ref, km_ref, kn_ref, vp_ref, vm_ref, vn_ref,
         kc_ref, vc_ref, o_ref) = refs
    else:
        sink_ref, q_ref, qg_ref, kc_ref, vc_ref, o_ref = refs
    h = pl.program_id(1)
    n = pl.program_id(2)
    qscale = HEAD_DIM ** -0.5 * LOG2E
    kc = kc_ref[0]
    vc = vc_ref[0]
    rows = ATT_GROUP * ATT_BLOCK
    if band:
        kwin = jnp.concatenate([kp_ref[0], km_ref[0], kn_ref[0]], axis=0)
        vwin = jnp.concatenate([vp_ref[0], vm_ref[0], vn_ref[0]], axis=0)
        ri = lax.broadcasted_iota(jnp.int32, (rows, 3 * ATT_BLOCK), 0) % ATT_BLOCK
        ci = lax.broadcasted_iota(jnp.int32, (rows, 3 * ATT_BLOCK), 1)
        band_bias = jnp.where(ci >= ri, jnp.where(ci <= ri + 2 * ATT_WINDOW, 0.0, NEG_INF), NEG_INF)
        col = lax.broadcasted_iota(jnp.int32, (1, 3 * ATT_BLOCK), 1)
    sinkcol = jnp.concatenate(
        [jnp.full((ATT_BLOCK, 1), sink_ref[h * ATT_GROUP + hh] * LOG2E, f32) for hh in range(ATT_GROUP)], axis=0)
    nt = (((1,), (1,)), ((), ()))
    nblocks = tq // ATT_BLOCK
    for g0 in range(0, nblocks, ATT_GROUP_BLOCKS):
        _attn_blocks(range(g0, min(g0 + ATT_GROUP_BLOCKS, nblocks)), locals())


def _attn_blocks(blocks, env):
    band, tq, seq, n, rows, nt, qscale = (env[k] for k in ("band", "tq", "seq", "n", "rows", "nt", "qscale"))
    q_ref, qg_ref, o_ref, kc, vc, sinkcol = (env[k] for k in ("q_ref", "qg_ref", "o_ref", "kc", "vc", "sinkcol"))
    if band:
        cos_ref, sin_ref, kwin, vwin, band_bias, col = (
            env[k] for k in ("cos_ref", "sin_ref", "kwin", "vwin", "band_bias", "col"))
    qs, s_c, s_w, m, p_c, p_w, den, o = {}, {}, {}, {}, {}, {}, {}, {}
    for jb in blocks:
        r0 = jb * ATT_BLOCK
        qparts = []
        for hh in range(ATT_GROUP):
            qv = q_ref[0, r0:r0 + ATT_BLOCK, hh * HEAD_DIM:(hh + 1) * HEAD_DIM].astype(f32)
            if band:
                qv = _norm_rope(qv, qg_ref[...], cos_ref[r0:r0 + ATT_BLOCK, :], sin_ref[r0:r0 + ATT_BLOCK, :])
            else:
                qv = _norm_rope(qv, qg_ref[...], None, None)
            qparts.append((qv * qscale).astype(bf16))
        qs[jb] = jnp.concatenate(qparts, axis=0)
    for jb in blocks:
        r0 = jb * ATT_BLOCK
        s_c[jb] = lax.dot_general(qs[jb], kc, nt, preferred_element_type=f32)
        if band:
            kpos = col + (n * tq + r0 - ATT_BLOCK)
            col_bias = jnp.where(kpos >= 0, jnp.where(kpos < seq, 0.0, NEG_INF), NEG_INF)
            s_w[jb] = (lax.dot_general(qs[jb], kwin[r0:r0 + 3 * ATT_BLOCK], nt, preferred_element_type=f32)
                       + (band_bias + col_bias))
    def lane_tiles(*arrs):
        return [a[:, c0:c0 + LANES] for a in arrs for c0 in range(0, a.shape[1], LANES)]

    nct = kc.shape[0] // LANES
    stiles = {}
    for jb in blocks:
        stiles[jb] = lane_tiles(s_c[jb], s_w[jb]) if band else lane_tiles(s_c[jb])
        m[jb] = jnp.maximum(jnp.max(functools.reduce(jnp.maximum, stiles[jb]), axis=-1, keepdims=True), sinkcol)
    for jb in blocks:
        mb = jnp.broadcast_to(m[jb], (rows, LANES))
        ptiles = [jnp.exp2(t - mb) for t in stiles[jb]]
        den[jb] = jnp.sum(functools.reduce(jnp.add, ptiles), axis=-1, keepdims=True) + jnp.exp2(sinkcol - m[jb])
        p_c[jb] = jnp.concatenate([t.astype(bf16) for t in ptiles[:nct]], axis=1)
        if band:
            p_w[jb] = jnp.concatenate([t.astype(bf16) for t in ptiles[nct:]], axis=1)
    for jb in blocks:
        r0 = jb * ATT_BLOCK
        o[jb] = jnp.dot(p_c[jb], vc, preferred_element_type=f32)
        if band:
            o[jb] = o[jb] + jnp.dot(p_w[jb], vwin[r0:r0 + 3 * ATT_BLOCK], preferred_element_type=f32)
    for jb in blocks:
        r0 = jb * ATT_BLOCK
        res = o[jb] * jnp.broadcast_to(1.0 / den[jb], (rows, HEAD_DIM))
        for hh in range(ATT_GROUP):
            o_ref[0, r0:r0 + ATT_BLOCK, hh * HEAD_DIM:(hh + 1) * HEAD_DIM] = (
                res[hh * ATT_BLOCK:(hh + 1) * ATT_BLOCK].astype(bf16))


def _attention(q, qg, sink, kc, vc, k=None, v=None, cos=None, sins=None):
    b, t, _ = q.shape
    lc = kc.shape[1]
    band = k is not None
    tq = ATT_TQ if t % ATT_TQ == 0 else 2 * ATT_BLOCK
    gw = ATT_GROUP * HEAD_DIM
    nblk = t // ATT_BLOCK
    per = tq // ATT_BLOCK
    smem = pl.BlockSpec(memory_space=pltpu.SMEM)
    qspec = pl.BlockSpec((1, tq, gw), lambda bi, h, n: (bi, n, h))
    gspec = pl.BlockSpec((1, HEAD_DIM), lambda bi, h, n: (0, 0))
    cspec = pl.BlockSpec((1, lc, HEAD_DIM), lambda bi, h, n: (bi, 0, h))
    if band:
        tab = pl.BlockSpec((tq, HEAD_DIM), lambda bi, h, n: (n, 0))
        prev = pl.BlockSpec((1, ATT_BLOCK, HEAD_DIM), lambda bi, h, n: (bi, jnp.maximum(n * per - 1, 0), h))
        main = pl.BlockSpec((1, tq, HEAD_DIM), lambda bi, h, n: (bi, n, h))
        nxt = pl.BlockSpec((1, ATT_BLOCK, HEAD_DIM), lambda bi, h, n: (bi, jnp.minimum((n + 1) * per, nblk - 1), h))
        in_specs = [smem, qspec, tab, tab, gspec, prev, main, nxt, prev, main, nxt, cspec, cspec]
        args = (sink, q, cos, sins, qg, k, k, k, v, v, v, kc, vc)
    else:
        in_specs = [smem, qspec, gspec, cspec, cspec]
        args = (sink, q, qg, kc, vc)
    return pl.pallas_call(
        functools.partial(_attn_kernel, band=band, tq=tq, seq=t),
        grid=(b, ATT_KV_HEADS, t // tq),
        in_specs=in_specs,
        out_specs=pl.BlockSpec((1, tq, gw), lambda bi, h, n: (bi, n, h)),
        out_shape=jax.ShapeDtypeStruct((b, t, ATT_W), bf16),
        compiler_params=_cparams(("arbitrary", "arbitrary", "arbitrary")),
        name="attn_band" if band else "attn_ctx",
    )(*args)


def _ssd_kernel(xc_ref, bc_ref, cc_ref, dtc_ref, zc_ref, xl_ref, bl_ref, cl_ref, dtl_ref, zl_ref,
                cw_ref, cb_ref, dtb_ref, a_ref, acol_ref, dsk_ref, ng_ref, oc_ref, ol_ref,
                pad_scr, xs_scr, cm_scr, bt_scr, dts_scr, dtt_scr, y_scr, ydir_scr, h_scr, *, lc, seq):
    ck = SSD_CHUNK
    gw = SSD_GW
    nst = SSD_STATE
    row = lax.broadcasted_iota(jnp.int32, (ck, ck), 0)
    col = lax.broadcasted_iota(jnp.int32, (ck, ck), 1)
    tri = (row >= col, col >= row)
    tri_bf = tuple(jnp.where(t, 1.0, 0.0).astype(bf16) for t in tri)
    first = col < SSD_HEAD_DIM
    h_scr[...] = jnp.zeros(h_scr.shape, f32)

    def run_seq(t, x_ref, b_ref, c_ref, dt_ref, z_ref, o_ref):
        nc = t // ck
        zero8 = jnp.zeros((8, gw + 2 * nst), f32)
        pad_scr[0:8, :] = zero8
        pad_scr[8 + t:16 + t, :] = zero8

        def fill(c, carry):
            r0 = pl.multiple_of(c * ck, ck)
            pad_scr[pl.ds(r0 + 8, ck), 0:gw] = x_ref[0, pl.ds(r0, ck), :].astype(f32)
            pad_scr[pl.ds(r0 + 8, ck), gw:gw + nst] = b_ref[0, pl.ds(r0, ck), :].astype(f32)
            pad_scr[pl.ds(r0 + 8, ck), gw + nst:gw + 2 * nst] = c_ref[0, pl.ds(r0, ck), :].astype(f32)
            return carry

        lax.fori_loop(0, nc, fill, 0)

        per = SSD_TRIP_CHUNKS if nc % SSD_TRIP_CHUNKS == 0 else 1

        def conv(c, carry):
            sub = range(per)
            cc = [c * per + t for t in sub]
            r0 = [pl.multiple_of(cc[t] * ck, ck) for t in sub]
            rows3 = [_shifted_rows(pad_scr, r0[t], ck) for t in sub]
            act = [_silu(cw_ref[0, 0:1, :] * rows3[t][0] + cw_ref[0, 1:2, :] * rows3[t][1]
                         + cw_ref[0, 2:3, :] * rows3[t][2] + cb_ref[0]) for t in sub]
            dv = [dt_ref[0, pl.ds(r0[t], ck), :] + dtb_ref[0] for t in sub]
            dts = [jnp.maximum(dv[t], 0.0) + jnp.log1p(jnp.exp(-jnp.abs(dv[t]))) for t in sub]
            for t in sub:
                xs = act[t][:, 0:gw]
                y_scr[pl.ds(r0[t], ck), :] = dsk_ref[...] * xs
                xs_scr[pl.ds(r0[t], ck), :] = xs.astype(bf16)
                bt_scr[cc[t]] = act[t][:, gw:gw + nst].T
                cm_scr[pl.ds(r0[t], ck), :] = act[t][:, gw + nst:gw + 2 * nst].astype(bf16)
                dts_scr[pl.ds(r0[t], ck), :] = dts[t]
                dtt_scr[cc[t]] = dts[t].T[0:DT_ROWS, :]
            return carry

        lax.fori_loop(0, nc // per, conv, 0)

        def chunk(i, carry):
            dirs = (0, 1)
            streams = [(t, d) for t in range(per) for d in dirs]
            cidx = {(t, d): (i * per + t if d == 0 else nc - 1 - (i * per + t)) for (t, d) in streams}
            r0 = {s: pl.multiple_of(cidx[s] * ck, ck) for s in streams}
            cb16 = {s: cm_scr[pl.ds(r0[s], ck), :] for s in streams}
            bt32 = {s: bt_scr[cidx[s]] for s in streams}
            dt = {s: dts_scr[pl.ds(r0[s], ck), :] for s in streams}
            dt_t = {s: dtt_scr[cidx[s]] for s in streams}
            sp = {s: _split3(dt[s] * a_ref[0]) for s in streams}
            sp_t = {s: _split3(dt_t[s] * acol_ref[0, 0]) for s in streams}
            c3 = {s: jnp.dot(tri_bf[s[1]], jnp.concatenate(sp[s], axis=1), preferred_element_type=f32)
                  for s in streams}
            cum = {s: c3[s][:, 0:LANES] + c3[s][:, LANES:2 * LANES] + c3[s][:, 2 * LANES:] for s in streams}
            c3t = {s: jnp.dot(jnp.concatenate(sp_t[s], axis=0), tri_bf[1 - s[1]], preferred_element_type=f32)
                   for s in streams}
            cum_t = {s: c3t[s][0:DT_ROWS] + c3t[s][DT_ROWS:2 * DT_ROWS] + c3t[s][2 * DT_ROWS:] for s in streams}
            cbm = {s: jnp.dot(cb16[s], bt32[s].astype(bf16), preferred_element_type=f32) for s in streams}
            tot = {s: (cum[s][ck - 1:ck, :] if s[1] == 0 else cum[s][0:1, :]) for s in streams}
            tot_t = {s: (cum_t[s][:, ck - 1:ck] if s[1] == 0 else cum_t[s][:, 0:1]) for s in streams}
            w_t = {s: dt_t[s] * jnp.exp(tot_t[s] - cum_t[s]) for s in streams}
            etot = {s: jnp.exp(tot[s]) for s in streams}
            pairs = range(SSD_KH // 2)
            units = [(t, p, d) for t in range(per) for p in pairs for d in dirs]
            xp16 = {(t, p, d): xs_scr[pl.ds(r0[t, d], ck), p * LANES:(p + 1) * LANES] for (t, p, d) in units}
            mks, ccols, bws = {}, {}, {}
            for kk in range(2):
                for (t, p, d) in units:
                    s = (t, d)
                    j = d * SSD_KH + 2 * p + kk
                    ccols[t, p, d, kk] = jnp.broadcast_to(cum[s][:, j:j + 1], (ck, ck))
                    decay = jnp.exp(jnp.where(tri[d], ccols[t, p, d, kk] - cum_t[s][j:j + 1, :], NEG_INF))
                    mks[t, p, d, kk] = (cbm[s] * decay * dt_t[s][j:j + 1, :]).astype(bf16)
                    bws[t, p, d, kk] = (bt32[s] * w_t[s][j:j + 1, :]).astype(bf16)
            yy, ss = {}, {}
            for u in units:
                yy[u] = jnp.dot(jnp.concatenate([mks[u + (0,)], mks[u + (1,)]], axis=0), xp16[u],
                                preferred_element_type=f32)
                ss[u] = jnp.dot(jnp.concatenate([bws[u + (0,)], bws[u + (1,)]], axis=0), xp16[u],
                                preferred_element_type=f32)
            hcur = {(p, d): h_scr[d * (SSD_KH // 2) + p] for p in pairs for d in dirs}
            for (t, p, d) in units:
                s, u = (t, d), (t, p, d)
                j0 = d * SSD_KH + 2 * p
                ch = jnp.dot(cb16[s], hcur[p, d].astype(bf16), preferred_element_type=f32)
                ydiag = jnp.where(first, yy[u][0:ck], yy[u][ck:])
                yoff = jnp.exp(jnp.where(first, ccols[u + (0,)], ccols[u + (1,)])) * ch
                ydir_scr[d, pl.ds(r0[s], ck), p * LANES:(p + 1) * LANES] = ydiag + yoff
                erow = jnp.where(first[0:1, :], etot[s][:, j0:j0 + 1], etot[s][:, j0 + 1:j0 + 2])
                hcur[p, d] = hcur[p, d] * erow + jnp.where(first, ss[u][0:nst], ss[u][nst:])
            for p in pairs:
                for d in dirs:
                    h_scr[d * (SSD_KH // 2) + p] = hcur[p, d]
            return carry

        lax.fori_loop(0, nc // per, chunk, 0)

        def gate(c, carry):
            sub = range(per)
            r0 = [pl.multiple_of((c * per + t) * ck, ck) for t in sub]
            y = [y_scr[pl.ds(r0[t], ck), :] + ydir_scr[0, pl.ds(r0[t], ck), :] + ydir_scr[1, pl.ds(r0[t], ck), :]
                 for t in sub]
            u = [y[t] * _silu(z_ref[0, pl.ds(r0[t], ck), :].astype(f32)) for t in sub]
            ms = [jnp.mean(u[t] * u[t], axis=-1, keepdims=True) for t in sub]
            u = [u[t] * lax.rsqrt(ms[t] + NORM_EPS) for t in sub]
            for t in sub:
                o_ref[0, pl.ds(r0[t], ck), :] = (u[t] * ng_ref[...]).astype(bf16)
            return carry

        lax.fori_loop(0, nc // per, gate, 0)

    run_seq(lc, xc_ref, bc_ref, cc_ref, dtc_ref, zc_ref, oc_ref)
    run_seq(seq, xl_ref, bl_ref, cl_ref, dtl_ref, zl_ref, ol_ref)


def _ssd(xbc_c, dt_c, z_c, xbc_l, dt_l, z_l, cw, cb, dtb, a_neg, a_col, dsk, ng):
    b, lc, _ = xbc_c.shape
    seq = xbc_l.shape[1]
    gw, nst = SSD_GW, SSD_STATE
    xoff = SSD_W // nst

    def seq_specs(t):
        return [pl.BlockSpec((1, t, gw), lambda bi, g: (bi, 0, g)),
                pl.BlockSpec((1, t, nst), lambda bi, g: (bi, 0, xoff + g)),
                pl.BlockSpec((1, t, nst), lambda bi, g: (bi, 0, xoff + SSD_GROUPS + g)),
                pl.BlockSpec((1, t, LANES), lambda bi, g: (bi, 0, g)),
                pl.BlockSpec((1, t, gw), lambda bi, g: (bi, 0, g))]

    cwid = gw + 2 * nst
    par_specs = [pl.BlockSpec((1, 3, cwid), lambda bi, g: (g, 0, 0)),
                 pl.BlockSpec((1, 1, cwid), lambda bi, g: (g, 0, 0)),
                 pl.BlockSpec((1, 1, LANES), lambda bi, g: (g, 0, 0)),
                 pl.BlockSpec((1, 1, LANES), lambda bi, g: (g, 0, 0)),
                 pl.BlockSpec((1, 1, DT_ROWS, 1), lambda bi, g: (g, 0, 0, 0)),
                 pl.BlockSpec((1, gw), lambda bi, g: (0, g)),
                 pl.BlockSpec((1, gw), lambda bi, g: (0, g))]
    nck = seq // SSD_CHUNK
    return pl.pallas_call(
        functools.partial(_ssd_kernel, lc=lc, seq=seq),
        grid=(b, SSD_GROUPS),
        in_specs=seq_specs(lc) + seq_specs(seq) + par_specs,
        out_specs=[pl.BlockSpec((1, lc, gw), lambda bi, g: (bi, 0, g)),
                   pl.BlockSpec((1, seq, gw), lambda bi, g: (bi, 0, g))],
        out_shape=[jax.ShapeDtypeStruct((b, lc, SSD_W), bf16), jax.ShapeDtypeStruct((b, seq, SSD_W), bf16)],
        scratch_shapes=[pltpu.VMEM((seq + 16, cwid), f32),
                        pltpu.VMEM((seq, gw), bf16),
                        pltpu.VMEM((seq, nst), bf16),
                        pltpu.VMEM((nck, nst, SSD_CHUNK), f32),
                        pltpu.VMEM((seq, LANES), f32),
                        pltpu.VMEM((nck, DT_ROWS, SSD_CHUNK), f32),
                        pltpu.VMEM((seq, gw), f32),
                        pltpu.VMEM((2, seq, gw), f32),
                        pltpu.VMEM((SSD_KH, nst, LANES), f32)],
        compiler_params=_cparams(("arbitrary", "arbitrary")),
        name="ssd",
    )(xbc_c, xbc_c, xbc_c, dt_c, z_c, xbc_l, xbc_l, xbc_l, dt_l, z_l, cw, cb, dtb, a_neg, a_col, dsk, ng)


def _sconv_kernel(b_ref, c_ref, h_ref, w_ref, o_ref, pad_scr, *, t):
    ck = min(256, t)
    zero8 = jnp.zeros((8, LANES), f32)
    pad_scr[0:8, :] = zero8
    pad_scr[8 + t:16 + t, :] = zero8

    def fill(c, carry):
        r0 = pl.multiple_of(c * ck, ck)
        pad_scr[pl.ds(r0 + 8, ck), :] = c_ref[0, pl.ds(r0, ck), :].astype(f32) * h_ref[0, pl.ds(r0, ck), :].astype(f32)
        return carry

    lax.fori_loop(0, t // ck, fill, 0)

    def conv(c, carry):
        r0 = pl.multiple_of(c * ck, ck)
        xm, x0, xp = _shifted_rows(pad_scr, r0, ck)
        acc = w_ref[0:1, :] * xm + w_ref[1:2, :] * x0 + w_ref[2:3, :] * xp
        o_ref[0, pl.ds(r0, ck), :] = (b_ref[0, pl.ds(r0, ck), :].astype(f32) * acc).astype(bf16)
        return carry

    lax.fori_loop(0, t // ck, conv, 0)


def _sconv(scb, scc, sch, w):
    b, t, cw = scb.shape
    spec = pl.BlockSpec((1, t, LANES), lambda bi, j: (bi, 0, j))
    return pl.pallas_call(
        functools.partial(_sconv_kernel, t=t),
        grid=(b, cw // LANES),
        in_specs=[spec, spec, spec, pl.BlockSpec((3, LANES), lambda bi, j: (0, j))],
        out_specs=spec,
        out_shape=jax.ShapeDtypeStruct((b, t, cw), bf16),
        scratch_shapes=[pltpu.VMEM((t + 16, LANES), f32)],
        compiler_params=_cparams(("arbitrary", "arbitrary")),
        name="sconv",
    )(scb, scc, sch, w)


def _outproj_kernel(att_ref, ssd_ref, sc_ref, x_ref, g1_ref, sh2_ref, sc2_ref, n2_ref, wo_ref, wr_ref,
                    xo_ref, h2_ref, lg_ref, *, sub):
    for r0 in range(0, x_ref.shape[0], sub):
        rs = slice(r0, r0 + sub)
        acc = jnp.dot(att_ref[rs, :], wo_ref[0, 0:ATT_W, :], preferred_element_type=f32)
        acc = acc + jnp.dot(ssd_ref[rs, :], wo_ref[0, ATT_W:ATT_W + SSD_W, :], preferred_element_type=f32)
        acc = acc + jnp.dot(sc_ref[rs, :], wo_ref[0, ATT_W + SSD_W:, :], preferred_element_type=f32)
        x = x_ref[rs, :] + g1_ref[0] * acc
        xo_ref[rs, :] = x
        y = x * lax.rsqrt(jnp.mean(x * x, axis=-1, keepdims=True) + NORM_EPS) * n2_ref[...]
        h2 = y * (1.0 + sc2_ref[0]) + sh2_ref[0]
        hh = h2.astype(bf16)
        h2_ref[rs, :] = hh
        hm = (h2 - hh.astype(f32)).astype(bf16)
        r1 = jnp.dot(hh, wr_ref[0], preferred_element_type=f32)
        r2 = jnp.dot(hm, wr_ref[0, :, 0:ROUTER_PAD], preferred_element_type=f32)
        lg_ref[rs, :] = r1[:, 0:ROUTER_PAD] + r1[:, ROUTER_PAD:] + r2


def _outproj(att, ssd, sconv, x2d, g1, sh2, sc2, n2, wo, wr2, layer, tm, rows_per_mod):
    m, d = x2d.shape
    tpm = rows_per_mod // tm
    mod_spec = pl.BlockSpec((1, 1, d), lambda i: (i // tpm, 0, 0))
    return pl.pallas_call(
        functools.partial(_outproj_kernel, sub=min(256, tm)),
        grid=(m // tm,),
        in_specs=[pl.BlockSpec((tm, ATT_W), lambda i: (i, 0)), pl.BlockSpec((tm, SSD_W), lambda i: (i, 0)),
                  pl.BlockSpec((tm, SC_W), lambda i: (i, 0)), pl.BlockSpec((tm, d), lambda i: (i, 0)),
                  mod_spec, mod_spec, mod_spec, pl.BlockSpec((1, d), lambda i: (0, 0)),
                  pl.BlockSpec((1, d, d), lambda i: (layer, 0, 0)),
                  pl.BlockSpec((1, d, 2 * ROUTER_PAD), lambda i: (layer, 0, 0))],
        out_specs=[pl.BlockSpec((tm, d), lambda i: (i, 0)), pl.BlockSpec((tm, d), lambda i: (i, 0)),
                   pl.BlockSpec((tm, ROUTER_PAD), lambda i: (i, 0))],
        out_shape=[jax.ShapeDtypeStruct((m, d), f32), jax.ShapeDtypeStruct((m, d), bf16),
                   jax.ShapeDtypeStruct((m, ROUTER_PAD), f32)],
        compiler_params=_cparams(("arbitrary",)),
        name="outproj",
    )(att, ssd, sconv, x2d, g1, sh2, sc2, n2, wo, wr2)


def _ffn_kernel(*refs, nt, with_ctx):
    if with_ctx:
        x_ref, gv_ref, xc_ref, gvc_ref, wg_ref, wu_ref, wd_ref, o_ref, oc_ref = refs
    else:
        x_ref, gv_ref, wg_ref, wu_ref, wd_ref, o_ref = refs

    def ffn(xr, gr, outr):
        x = xr[0]
        a = jnp.dot(x, wg_ref[0, 0].astype(bf16), preferred_element_type=f32)
        u = jnp.dot(x, wu_ref[0, 0].astype(bf16), preferred_element_type=f32)
        hmid = (_silu(a) * u).astype(bf16)
        outr[0] = (jnp.dot(hmid, wd_ref[0, 0].astype(bf16), preferred_element_type=f32) * gr[0]).astype(bf16)

    if not with_ctx:
        ffn(x_ref, gv_ref, o_ref)
        return
    i = pl.program_id(1)

    @pl.when(i < nt)
    def _():
        ffn(x_ref, gv_ref, o_ref)

    @pl.when(i == nt)
    def _():
        ffn(xc_ref, gvc_ref, oc_ref)


def _ffn_tile(nrows):
    for nt in range(1, nrows + 1):
        if nrows % nt == 0 and (nrows // nt) % 16 == 0 and nrows // nt <= FFN_MAX_TILE:
            return nrows // nt
    raise ValueError(f"no aligned row tile for {nrows} gathered rows")


def _expert_ffn(xg, gv, wg, wu, wd, layer, xgc=None, gvc=None):
    e, r, d = xg.shape
    ff = wg.shape[3]
    tm = _ffn_tile(r)
    nt = r // tm
    with_ctx = xgc is not None
    last = nt - 1
    row_specs = [pl.BlockSpec((1, tm, d), lambda ei, i: (ei, jnp.minimum(i, last), 0)),
                 pl.BlockSpec((1, tm, 1), lambda ei, i: (ei, jnp.minimum(i, last), 0))]
    one = pl.Buffered(1)
    w_specs = [pl.BlockSpec((1, 1, d, ff), lambda ei, i: (layer, ei, 0, 0), pipeline_mode=one),
               pl.BlockSpec((1, 1, d, ff), lambda ei, i: (layer, ei, 0, 0)),
               pl.BlockSpec((1, 1, ff, d), lambda ei, i: (layer, ei, 0, 0))]
    out_specs = [pl.BlockSpec((1, tm, d), lambda ei, i: (ei, jnp.minimum(i, last), 0))]
    out_shape = [jax.ShapeDtypeStruct((e, r, d), bf16)]
    args = [xg, gv]
    if with_ctx:
        rc = xgc.shape[1]
        row_specs += [pl.BlockSpec((1, rc, d), lambda ei, i: (ei, 0, 0)),
                      pl.BlockSpec((1, rc, 1), lambda ei, i: (ei, 0, 0))]
        out_specs.append(pl.BlockSpec((1, rc, d), lambda ei, i: (ei, 0, 0)))
        out_shape.append(jax.ShapeDtypeStruct((e, rc, d), bf16))
        args += [xgc, gvc]
    outs = pl.pallas_call(
        functools.partial(_ffn_kernel, nt=nt, with_ctx=with_ctx),
        grid=(e, nt + (1 if with_ctx else 0)),
        in_specs=row_specs + w_specs,
        out_specs=out_specs,
        out_shape=out_shape,
        compiler_params=_cparams(("arbitrary", "arbitrary")),
        name="expert_ffn",
    )(*args, wg, wu, wd)
    return outs if with_ctx else outs[0]


def _combine_kernel(hit_ref, idx_ref, y_ref, xm_ref, g2_ref, o_ref, acc_scr, *, sb):
    bi = pl.program_id(0)
    k = pl.program_id(1)
    e = pl.program_id(2)
    nb, nk, ne = pl.num_programs(0), pl.num_programs(1), pl.num_programs(2)
    tt = acc_scr.shape[0]
    nsb = y_ref.shape[1] // sb

    @pl.when(e == 0)
    def _():
        acc_scr[...] = jnp.zeros(acc_scr.shape, f32)

    tok = lax.broadcasted_iota(jnp.int32, (tt, sb), 0) + k * tt
    for j in range(nsb):
        @pl.when(hit_ref[((e * nb + bi) * nk + k) * nsb + j] != 0)
        def _():
            onehot = jnp.where(tok == idx_ref[0, 0, :, j * sb:(j + 1) * sb], 1.0, 0.0).astype(bf16)
            acc_scr[...] += jnp.dot(onehot, y_ref[0, j * sb:(j + 1) * sb, :], preferred_element_type=f32)

    @pl.when(e == ne - 1)
    def _():
        o_ref[...] = xm_ref[...] + g2_ref[0] * acc_scr[...]


def _combine(idx, y, xm, g2, b, t, per_sample_gate):
    e, _, _, cap = idx.shape
    d = xm.shape[1]
    tt = min(COMBINE_TILE, t)
    nk = t // tt
    sb = min(COMBINE_SLOTS, cap)
    nsb = cap // sb
    blk = idx.reshape(e, b, nsb, sb)
    lo = blk[..., 0][:, :, None, :]
    hi = blk[..., sb - 1][:, :, None, :]
    tile0 = (jnp.arange(nk, dtype=idx.dtype) * tt)[None, None, :, None]
    hit = ((lo < tile0 + tt) & (hi >= tile0)).astype(jnp.int32).reshape(-1)
    grid_spec = pltpu.PrefetchScalarGridSpec(
        num_scalar_prefetch=1,
        grid=(b, nk, e),
        in_specs=[pl.BlockSpec((1, 1, 1, cap), lambda bi, k, ei, hit_ref: (ei, bi, 0, 0)),
                  pl.BlockSpec((1, cap, d), lambda bi, k, ei, hit_ref: (ei, bi, 0)),
                  pl.BlockSpec((tt, d), lambda bi, k, ei, hit_ref: (bi * nk + k, 0)),
                  pl.BlockSpec((1, 1, d), lambda bi, k, ei, hit_ref: (bi if per_sample_gate else 0, 0, 0))],
        out_specs=pl.BlockSpec((tt, d), lambda bi, k, ei, hit_ref: (bi * nk + k, 0)),
        scratch_shapes=[pltpu.VMEM((tt, d), f32)])
    return pl.pallas_call(
        functools.partial(_combine_kernel, sb=sb),
        grid_spec=grid_spec,
        out_shape=jax.ShapeDtypeStruct(xm.shape, f32),
        compiler_params=_cparams(("arbitrary", "arbitrary", "arbitrary")),
        name="moe_combine",
    )(hit, idx, y, xm, g2)


def _rope_tables(n_tokens):
    rows = n_tokens // GRID_W
    row = jnp.repeat(jnp.arange(rows), GRID_W).astype(f32)
    col = jnp.tile(jnp.arange(GRID_W), rows).astype(f32)
    n_freq = HEAD_DIM // 4
    inv = ROPE_THETA ** (-jnp.arange(n_freq, dtype=f32) / n_freq)
    ang_r = row[:, None] * inv
    ang_c = col[:, None] * inv
    cos = jnp.concatenate([jnp.cos(ang_r), jnp.cos(ang_r), jnp.cos(ang_c), jnp.cos(ang_c)], axis=-1)
    sins = jnp.concatenate([-jnp.sin(ang_r), jnp.sin(ang_r), -jnp.sin(ang_c), jnp.sin(ang_c)], axis=-1)
    return cos, sins


def _dt_layout(v):
    lead = v.shape[:-1]
    v = v.reshape(lead + (2, SSD_GROUPS, SSD_KH))
    v = jnp.moveaxis(v, -2, -3).reshape(lead + (SSD_GROUPS, 2 * SSD_KH))
    v = jnp.pad(v, [(0, 0)] * (len(lead) + 1) + [(0, LANES - 2 * SSD_KH)])
    return v.reshape(lead + (DT_PAD,))


def _group_layout(v):
    outs = []
    for g in range(SSD_GROUPS):
        outs.append(jnp.concatenate([
            v[..., g * SSD_GW:(g + 1) * SSD_GW],
            v[..., SSD_W + g * SSD_STATE:SSD_W + (g + 1) * SSD_STATE],
            v[..., SSD_W + SSD_BC_W + g * SSD_STATE:SSD_W + SSD_BC_W + (g + 1) * SSD_STATE]], axis=-1))
    return jnp.stack(outs)


def _route(logits, b, t):
    cap = EC_FACTOR * t // N_EXPERTS
    aff = jax.nn.softmax(logits[:, :N_EXPERTS].reshape(b, t, N_EXPERTS), axis=-1)
    gval, idx = lax.top_k(jnp.swapaxes(aff, 1, 2), cap)
    idx, gval = lax.sort((idx, gval), dimension=-1, num_keys=1)
    rows = idx + (jnp.arange(b, dtype=idx.dtype) * t)[:, None, None]
    to_e = lambda u: jnp.swapaxes(u, 0, 1).reshape(N_EXPERTS, b * cap)
    return to_e(gval)[..., None], to_e(rows), jnp.swapaxes(idx, 0, 1).reshape(N_EXPERTS, b, 1, cap)


def kernel(x, c, ctx, c_ctx, norm1_g, norm2_g, w_mod, b_mod, w_in, q_norm_g, k_norm_g, attn_sink, ssd_conv_w,
           ssd_conv_b, ssd_dt_bias, ssd_a_log, ssd_d, ssd_norm_g, sc_conv_w, w_out, w_router, w_expert_gate,
           w_expert_up, w_expert_down):
    b, s, d = x.shape
    lc = ctx.shape[1]
    nl = w_in.shape[0]
    cos, sins = _rope_tables(s)

    nrow = -(-(b + 1) // 8) * 8
    cvec = jnp.zeros((nrow, d), f32).at[:b].set(c).at[b].set(c_ctx)
    mods = _modulations(cvec, w_mod, b_mod).reshape(nl, nrow, N_MOD, 1, d)

    cuts = [0]
    for wdt in (ATT_W, ATT_KV_W, ATT_KV_W, SSD_W, SSD_XBC_W, SSD_DT_W, SC_W, SC_W, SC_W):
        cuts.append(cuts[-1] + wdt)
    w_dt = _dt_layout(w_in[:, :, cuts[5]:cuts[6]])
    w_in_p = jnp.concatenate([w_in[:, :, :cuts[5]], w_dt, w_in[:, :, cuts[6]:]], axis=-1).astype(bf16)
    w_out_b = w_out.astype(bf16)
    wr_pad = jnp.pad(w_router, ((0, 0), (0, 0), (0, ROUTER_PAD - N_EXPERTS)))
    wr_hi, wr_mid, _ = _split3(wr_pad)
    wr2 = jnp.concatenate([wr_hi, wr_mid], axis=-1)
    dtb = _dt_layout(ssd_dt_bias.reshape(nl, SSD_DT_W)).reshape(nl, SSD_GROUPS, 1, LANES)
    a_neg = _dt_layout(-jnp.exp(ssd_a_log.reshape(nl, SSD_DT_W))).reshape(nl, SSD_GROUPS, 1, LANES)
    a_col = a_neg[..., :DT_ROWS].reshape(nl, SSD_GROUPS, 1, DT_ROWS, 1)
    dsk = jnp.repeat(ssd_d, SSD_HEAD_DIM, axis=-1).reshape(nl, 1, SSD_W)

    xl = x.reshape(b * s, d)
    xc = ctx.reshape(b * lc, d)
    for i in range(nl):
        with_ctx_out = i < nl - 1
        sh1, sc1, g1, sh2, sc2, g2 = (mods[i, :, j] for j in range(N_MOD))
        n1 = norm1_g[i].reshape(1, d)
        n2 = norm2_g[i].reshape(1, d)
        qg = q_norm_g[i].reshape(1, HEAD_DIM)
        kg = k_norm_g[i].reshape(1, HEAD_DIM)

        q, k, v, z, xbc, dtr, scb, scc, sch = _inproj(xl, sh1, sc1, n1, w_in_p, i, 512, s)
        qc, kc, vc, zc, xbcc, dtrc, scbc, sccc, schc = _inproj(xc, sh1[b:], sc1[b:], n1, w_in_p, i, lc, b * lc)

        kp = _kprep(k, kg, cos, sins, s, True)
        kcp = _kprep(kc, kg, cos, sins, lc, False)
        r3 = lambda u, t: u.reshape(b, t, u.shape[-1])
        att = _attention(r3(q, s), qg, attn_sink[i], r3(kcp, lc), r3(vc, lc), r3(kp, s), r3(v, s), cos, sins)

        ssd_c, ssd_l = _ssd(r3(xbcc, lc), r3(dtrc, lc), r3(zc, lc), r3(xbc, s), r3(dtr, s), r3(z, s),
                            _group_layout(ssd_conv_w[i]), _group_layout(ssd_conv_b[i].reshape(1, -1)),
                            dtb[i], a_neg[i], a_col[i], dsk[i], ssd_norm_g[i].reshape(1, SSD_W))
        sconv = _sconv(r3(scb, s), r3(scc, s), r3(sch, s), sc_conv_w[i])

        xm, h2, lg = _outproj(att.reshape(b * s, ATT_W), ssd_l.reshape(b * s, SSD_W), sconv.reshape(b * s, SC_W),
                              xl, g1, sh2, sc2, n2, w_out_b, wr2, i, 512, s)
        gv, rows, idx = _route(lg, b, s)
        if with_ctx_out:
            attc = _attention(r3(qc, lc), qg, attn_sink[i], r3(kcp, lc), r3(vc, lc))
            sconvc = _sconv(r3(scbc, lc), r3(sccc, lc), r3(schc, lc), sc_conv_w[i])
            xmc, h2c, lgc = _outproj(attc.reshape(b * lc, ATT_W), ssd_c.reshape(b * lc, SSD_W),
                                     sconvc.reshape(b * lc, SC_W), xc, g1[b:], sh2[b:], sc2[b:], n2,
                                     w_out_b, wr2, i, lc, b * lc)
            gvc, rowsc, idxc = _route(lgc, b, lc)
            y, yc = _expert_ffn(h2[rows], gv, w_expert_gate, w_expert_up, w_expert_down, i, h2c[rowsc], gvc)
            xc = _combine(idxc, yc, xmc, g2[b:], b, lc, False)
        else:
            y = _expert_ffn(h2[rows], gv, w_expert_gate, w_expert_up, w_expert_down, i)
        xl = _combine(idx, y, xm, g2, b, s, True)
    return xl.reshape(b, s, d)
```

```python
import functools

import jax
import jax.numpy as jnp
from jax import lax
from jax.experimental import pallas as pl
from jax.experimental.pallas import tpu as pltpu

f32 = jnp.float32
bf16 = jnp.bfloat16

D_MODEL = 2048
DEPTH = 4
GRID_W = 64
NORM_EPS = 1e-6
N_MOD = 6
HEAD_DIM = 128
ATT_HEADS = 8
ATT_KV_HEADS = 2
ATT_GROUP = ATT_HEADS // ATT_KV_HEADS
ATT_WINDOW = 128
ATT_BLOCK = 128
ROPE_THETA = 10000.0
SSD_HEAD_DIM = 64
SSD_W = 512
SSD_HEADS = 8
SSD_GROUPS = 2
SSD_KH = SSD_HEADS // SSD_GROUPS
SSD_STATE = 128
SSD_CHUNK = 128
SSD_GW = SSD_W // SSD_GROUPS
SC_W = 512
ATT_W = ATT_HEADS * HEAD_DIM
ATT_KV_W = ATT_KV_HEADS * HEAD_DIM
SSD_BC_W = SSD_GROUPS * SSD_STATE
SSD_XBC_W = SSD_W + 2 * SSD_BC_W
SSD_DT_W = 2 * SSD_HEADS
N_EXPERTS = 16
EC_FACTOR = 2
EXPERT_FF = 1024

LANES = 128
DT_PAD = SSD_GROUPS * LANES
ROUTER_PAD = LANES
SEG_W = (ATT_W, ATT_KV_W, ATT_KV_W, SSD_W, SSD_XBC_W, DT_PAD, SC_W, SC_W, SC_W)
SEG_DT = (bf16, bf16, bf16, bf16, bf16, f32, bf16, bf16, bf16)
IN_W_PAD = sum(SEG_W)
VMEM_LIMIT = 58 * 1024 * 1024
NEG_INF = float("-inf")
LOG2E = 1.4426950408889634
ATT_TQ = 1024
ATT_GROUP_BLOCKS = 8
COMBINE_TILE = 1024
COMBINE_SLOTS = 256
DT_ROWS = 16
SSD_TRIP_CHUNKS = 2
SSD_ELEMENTWISE_TRIP_CHUNKS = 4
KPREP_TILE = 2048
FFN_MAX_TILE = 576


def _cparams(sem):
    return pltpu.CompilerParams(dimension_semantics=sem, vmem_limit_bytes=VMEM_LIMIT)


def _silu(v):
    return v * jax.nn.sigmoid(v)


def _shifted_rows(pad_ref, r0, n):
    x0 = pad_ref[pl.ds(r0 + 8, n), :]
    before = pad_ref[pl.ds(r0, 8), :][7:8, :]
    after = pad_ref[pl.ds(r0 + 8 + n, 8), :][0:1, :]
    ri = lax.broadcasted_iota(jnp.int32, x0.shape, 0)
    xm = jnp.where(ri == 0, before, pltpu.roll(x0, 1, 0))
    xp = jnp.where(ri == n - 1, after, pltpu.roll(x0, n - 1, 0))
    return xm, x0, xp


def _split3(v):
    hi = v.astype(bf16)
    r = v - hi.astype(f32)
    mid = r.astype(bf16)
    lo = (r - mid.astype(f32)).astype(bf16)
    return hi, mid, lo


def _mod_kernel(c_ref, w_ref, b_ref, o_ref):
    a = _silu(c_ref[...]).astype(bf16)
    o_ref[0] = jnp.dot(a, w_ref[0].astype(bf16), preferred_element_type=f32) + b_ref[0]


def _modulations(cvec, w_mod, b_mod):
    nl, d, n = w_mod.shape
    r = cvec.shape[0]
    tn = 1024
    return pl.pallas_call(
        _mod_kernel,
        grid=(nl, n // tn),
        in_specs=[pl.BlockSpec((r, d), lambda l, j: (0, 0)),
                  pl.BlockSpec((1, d, tn), lambda l, j: (l, 0, j)),
                  pl.BlockSpec((1, 1, tn), lambda l, j: (l, 0, j))],
        out_specs=pl.BlockSpec((1, r, tn), lambda l, j: (l, 0, j)),
        out_shape=jax.ShapeDtypeStruct((nl, r, n), f32),
        compiler_params=_cparams(("arbitrary", "arbitrary")),
        name="modulations",
    )(cvec, w_mod, b_mod.reshape(nl, 1, n))


def _inproj_kernel(x_ref, sh_ref, sc_ref, g_ref, w_ref, *rest):
    outs, h_scr = rest[:-1], rest[-1]
    x = x_ref[...]
    y = x * lax.rsqrt(jnp.mean(x * x, axis=-1, keepdims=True) + NORM_EPS) * g_ref[...]
    h_scr[...] = (y * (1.0 + sc_ref[0]) + sh_ref[0]).astype(bf16)
    off = 0
    for ref, width in zip(outs, SEG_W):
        for c0 in range(0, width, 512):
            cw = min(512, width - c0)
            ref[:, c0:c0 + cw] = jnp.dot(h_scr[...], w_ref[0, :, off + c0:off + c0 + cw],
                                         preferred_element_type=f32).astype(ref.dtype)
        off += width


def _inproj(x2d, shift, scale, g, w, layer, tm, rows_per_mod):
    m, d = x2d.shape
    tpm = rows_per_mod // tm
    mod_spec = pl.BlockSpec((1, 1, d), lambda i: (i // tpm, 0, 0))
    return pl.pallas_call(
        _inproj_kernel,
        grid=(m // tm,),
        in_specs=[pl.BlockSpec((tm, d), lambda i: (i, 0)), mod_spec, mod_spec,
                  pl.BlockSpec((1, d), lambda i: (0, 0)),
                  pl.BlockSpec((1, d, IN_W_PAD), lambda i: (layer, 0, 0), pipeline_mode=pl.Buffered(1))],
        out_specs=[pl.BlockSpec((tm, wd), lambda i: (i, 0)) for wd in SEG_W],
        out_shape=[jax.ShapeDtypeStruct((m, wd), dt) for wd, dt in zip(SEG_W, SEG_DT)],
        scratch_shapes=[pltpu.VMEM((tm, d), bf16)],
        compiler_params=_cparams(("arbitrary",)),
        name="inproj",
    )(x2d, shift, scale, g, w)


def _norm_rope(v, g, cos, sins):
    y = v * lax.rsqrt(jnp.mean(v * v, axis=-1, keepdims=True) + NORM_EPS) * g
    if cos is None:
        return y
    lane = lax.broadcasted_iota(jnp.int32, y.shape, 1)
    quarter = HEAD_DIM // 4
    partner = jnp.where((lane % (2 * quarter)) < quarter,
                        pltpu.roll(y, HEAD_DIM - quarter, 1), pltpu.roll(y, quarter, 1))
    return y * cos + partner * sins


def _kprep_kernel(k_ref, g_ref, cos_ref, sin_ref, o_ref, *, rope):
    for hh in range(ATT_KV_HEADS):
        sl = slice(hh * HEAD_DIM, (hh + 1) * HEAD_DIM)
        v = k_ref[:, sl].astype(f32)
        o_ref[:, sl] = _norm_rope(v, g_ref[...], cos_ref[...] if rope else None,
                                  sin_ref[...] if rope else None).astype(bf16)


def _kprep(k2d, g, cos, sins, seq, rope):
    m = k2d.shape[0]
    tk = min(KPREP_TILE, seq)
    nt = seq // tk
    tab = pl.BlockSpec((tk, HEAD_DIM), lambda i: (i % nt, 0))
    return pl.pallas_call(
        functools.partial(_kprep_kernel, rope=rope),
        grid=(m // tk,),
        in_specs=[pl.BlockSpec((tk, ATT_KV_W), lambda i: (i, 0)),
                  pl.BlockSpec((1, HEAD_DIM), lambda i: (0, 0)), tab, tab],
        out_specs=pl.BlockSpec((tk, ATT_KV_W), lambda i: (i, 0)),
        out_shape=jax.ShapeDtypeStruct((m, ATT_KV_W), bf16),
        compiler_params=_cparams(("arbitrary",)),
        name="kprep_rope" if rope else "kprep",
    )(k2d, g, cos, sins)


def _attn_kernel(*refs, band, tq, seq):
    if band:
        (sink_ref, q_ref, cos_ref, sin_ref, qg_ref, kp_ref, km_ref, kn_ref, vp_ref, vm_ref, vn_ref,
         kc_ref, vc_ref, o_ref) = refs
    else:
        sink_ref, q_ref, qg_ref, kc_ref, vc_ref, o_ref = refs
    h = pl.program_id(1)
    n = pl.program_id(2)
    qscale = HEAD_DIM ** -0.5 * LOG2E
    kc = kc_ref[0]
    vc = vc_ref[0]
    rows = ATT_GROUP * ATT_BLOCK
    if band:
        kwin = jnp.concatenate([kp_ref[0], km_ref[0], kn_ref[0]], axis=0)
        vwin = jnp.concatenate([vp_ref[0], vm_ref[0], vn_ref[0]], axis=0)
        ri = lax.broadcasted_iota(jnp.int32, (rows, 3 * ATT_BLOCK), 0) % ATT_BLOCK
        ci = lax.broadcasted_iota(jnp.int32, (rows, 3 * ATT_BLOCK), 1)
        band_bias = jnp.where(ci >= ri, jnp.where(ci <= ri + 2 * ATT_WINDOW, 0.0, NEG_INF), NEG_INF)
        col = lax.broadcasted_iota(jnp.int32, (1, 3 * ATT_BLOCK), 1)
    sinkcol = jnp.concatenate(
        [jnp.full((ATT_BLOCK, 1), sink_ref[h * ATT_GROUP + hh] * LOG2E, f32) for hh in range(ATT_GROUP)], axis=0)
    nt = (((1,), (1,)), ((), ()))
    nblocks = tq // ATT_BLOCK
    for g0 in range(0, nblocks, ATT_GROUP_BLOCKS):
        _attn_blocks(range(g0, min(g0 + ATT_GROUP_BLOCKS, nblocks)), locals())


def _attn_blocks(blocks, env):
    band, tq, seq, n, rows, nt, qscale = (env[k] for k in ("band", "tq", "seq", "n", "rows", "nt", "qscale"))
    q_ref, qg_ref, o_ref, kc, vc, sinkcol = (env[k] for k in ("q_ref", "qg_ref", "o_ref", "kc", "vc", "sinkcol"))
    if band:
        cos_ref, sin_ref, kwin, vwin, band_bias, col = (
            env[k] for k in ("cos_ref", "sin_ref", "kwin", "vwin", "band_bias", "col"))
    qs, s_c, s_w, m, p_c, p_w, den, o = {}, {}, {}, {}, {}, {}, {}, {}
    for jb in blocks:
        r0 = jb * ATT_BLOCK
        qparts = []
        for hh in range(ATT_GROUP):
            qv = q_ref[0, r0:r0 + ATT_BLOCK, hh * HEAD_DIM:(hh + 1) * HEAD_DIM].astype(f32)
            if band:
                qv = _norm_rope(qv, qg_ref[...], cos_ref[r0:r0 + ATT_BLOCK, :], sin_ref[r0:r0 + ATT_BLOCK, :])
            else:
                qv = _norm_rope(qv, qg_ref[...], None, None)
            qparts.append((qv * qscale).astype(bf16))
        qs[jb] = jnp.concatenate(qparts, axis=0)
    for jb in blocks:
        r0 = jb * ATT_BLOCK
        s_c[jb] = lax.dot_general(qs[jb], kc, nt, preferred_element_type=f32)
        if band:
            kpos = col + (n * tq + r0 - ATT_BLOCK)
            col_bias = jnp.where(kpos >= 0, jnp.where(kpos < seq, 0.0, NEG_INF), NEG_INF)
            s_w[jb] = (lax.dot_general(qs[jb], kwin[r0:r0 + 3 * ATT_BLOCK], nt, preferred_element_type=f32)
                       + (band_bias + col_bias))
    def lane_tiles(*arrs):
        return [a[:, c0:c0 + LANES] for a in arrs for c0 in range(0, a.shape[1], LANES)]

    nct = kc.shape[0] // LANES
    stiles = {}
    for jb in blocks:
        stiles[jb] = lane_tiles(s_c[jb], s_w[jb]) if band else lane_tiles(s_c[jb])
        m[jb] = jnp.maximum(jnp.max(functools.reduce(jnp.maximum, stiles[jb]), axis=-1, keepdims=True), sinkcol)
    for jb in blocks:
        mb = jnp.broadcast_to(m[jb], (rows, LANES))
        ptiles = [jnp.exp2(t - mb) for t in stiles[jb]]
        den[jb] = jnp.sum(functools.reduce(jnp.add, ptiles), axis=-1, keepdims=True) + jnp.exp2(sinkcol - m[jb])
        p_c[jb] = jnp.concatenate([t.astype(bf16) for t in ptiles[:nct]], axis=1)
        if band:
            p_w[jb] = jnp.concatenate([t.astype(bf16) for t in ptiles[nct:]], axis=1)
    for jb in blocks:
        r0 = jb * ATT_BLOCK
        o[jb] = jnp.dot(p_c[jb], vc, preferred_element_type=f32)
        if band:
            o[jb] = o[jb] + jnp.dot(p_w[jb], vwin[r0:r0 + 3 * ATT_BLOCK], preferred_element_type=f32)
    for jb in blocks:
        r0 = jb * ATT_BLOCK
        res = o[jb] * jnp.broadcast_to(1.0 / den[jb], (rows, HEAD_DIM))
        for hh in range(ATT_GROUP):
            o_ref[0, r0:r0 + ATT_BLOCK, hh * HEAD_DIM:(hh + 1) * HEAD_DIM] = (
                res[hh * ATT_BLOCK:(hh + 1) * ATT_BLOCK].astype(bf16))


def _attention(q, qg, sink, kc, vc, k=None, v=None, cos=None, sins=None):
    b, t, _ = q.shape
    lc = kc.shape[1]
    band = k is not None
    tq = ATT_TQ if t % ATT_TQ == 0 else 2 * ATT_BLOCK
    gw = ATT_GROUP * HEAD_DIM
    nblk = t // ATT_BLOCK
    per = tq // ATT_BLOCK
    smem = pl.BlockSpec(memory_space=pltpu.SMEM)
    qspec = pl.BlockSpec((1, tq, gw), lambda bi, h, n: (bi, n, h))
    gspec = pl.BlockSpec((1, HEAD_DIM), lambda bi, h, n: (0, 0))
    cspec = pl.BlockSpec((1, lc, HEAD_DIM), lambda bi, h, n: (bi, 0, h))
    if band:
        tab = pl.BlockSpec((tq, HEAD_DIM), lambda bi, h, n: (n, 0))
        prev = pl.BlockSpec((1, ATT_BLOCK, HEAD_DIM), lambda bi, h, n: (bi, jnp.maximum(n * per - 1, 0), h))
        main = pl.BlockSpec((1, tq, HEAD_DIM), lambda bi, h, n: (bi, n, h))
        nxt = pl.BlockSpec((1, ATT_BLOCK, HEAD_DIM), lambda bi, h, n: (bi, jnp.minimum((n + 1) * per, nblk - 1), h))
        in_specs = [smem, qspec, tab, tab, gspec, prev, main, nxt, prev, main, nxt, cspec, cspec]
        args = (sink, q, cos, sins, qg, k, k, k, v, v, v, kc, vc)
    else:
        in_specs = [smem, qspec, gspec, cspec, cspec]
        args = (sink, q, qg, kc, vc)
    return pl.pallas_call(
        functools.partial(_attn_kernel, band=band, tq=tq, seq=t),
        grid=(b, ATT_KV_HEADS, t // tq),
        in_specs=in_specs,
        out_specs=pl.BlockSpec((1, tq, gw), lambda bi, h, n: (bi, n, h)),
        out_shape=jax.ShapeDtypeStruct((b, t, ATT_W), bf16),
        compiler_params=_cparams(("arbitrary", "arbitrary", "arbitrary")),
        name="attn_band" if band else "attn_ctx",
    )(*args)


def _ssd_kernel(xc_ref, bc_ref, cc_ref, dtc_ref, zc_ref, xl_ref, bl_ref, cl_ref, dtl_ref, zl_ref,
                cw_ref, cb_ref, dtb_ref, a_ref, acol_ref, dsk_ref, ng_ref, oc_ref, ol_ref,
                pad_scr, xs_scr, cm_scr, bt_scr, dts_scr, dtt_scr, y_scr, ydir_scr, h_scr, *, lc, seq):
    ck = SSD_CHUNK
    gw = SSD_GW
    nst = SSD_STATE
    row = lax.broadcasted_iota(jnp.int32, (ck, ck), 0)
    col = lax.broadcasted_iota(jnp.int32, (ck, ck), 1)
    tri = (row >= col, col >= row)
    tri_bf = tuple(jnp.where(t, 1.0, 0.0).astype(bf16) for t in tri)
    first = col < SSD_HEAD_DIM
    h_scr[...] = jnp.zeros(h_scr.shape, f32)

    def run_seq(t, x_ref, b_ref, c_ref, dt_ref, z_ref, o_ref):
        nc = t // ck
        zero8 = jnp.zeros((8, gw + 2 * nst), f32)
        pad_scr[0:8, :] = zero8
        pad_scr[8 + t:16 + t, :] = zero8

        def fill(c, carry):
            r0 = pl.multiple_of(c * ck, ck)
            pad_scr[pl.ds(r0 + 8, ck), 0:gw] = x_ref[0, pl.ds(r0, ck), :].astype(f32)
            pad_scr[pl.ds(r0 + 8, ck), gw:gw + nst] = b_ref[0, pl.ds(r0, ck), :].astype(f32)
            pad_scr[pl.ds(r0 + 8, ck), gw + nst:gw + 2 * nst] = c_ref[0, pl.ds(r0, ck), :].astype(f32)
            return carry

        lax.fori_loop(0, nc, fill, 0)

        per = SSD_TRIP_CHUNKS if nc % SSD_TRIP_CHUNKS == 0 else 1
        eper = SSD_ELEMENTWISE_TRIP_CHUNKS if nc % SSD_ELEMENTWISE_TRIP_CHUNKS == 0 else per

        def conv(c, carry):
            sub = range(eper)
            cc = [c * eper + t for t in sub]
            r0 = [pl.multiple_of(cc[t] * ck, ck) for t in sub]
            rows3 = [_shifted_rows(pad_scr, r0[t], ck) for t in sub]
            act = [_silu(cw_ref[0, 0:1, :] * rows3[t][0] + cw_ref[0, 1:2, :] * rows3[t][1]
                         + cw_ref[0, 2:3, :] * rows3[t][2] + cb_ref[0]) for t in sub]
            dv = [dt_ref[0, pl.ds(r0[t], ck), :] + dtb_ref[0] for t in sub]
            dts = [jnp.maximum(dv[t], 0.0) + jnp.log1p(jnp.exp(-jnp.abs(dv[t]))) for t in sub]
            for t in sub:
                xs = act[t][:, 0:gw]
                y_scr[pl.ds(r0[t], ck), :] = dsk_ref[...] * xs
                xs_scr[pl.ds(r0[t], ck), :] = xs.astype(bf16)
                bt_scr[cc[t]] = act[t][:, gw:gw + nst].T
                cm_scr[pl.ds(r0[t], ck), :] = act[t][:, gw + nst:gw + 2 * nst].astype(bf16)
                dts_scr[pl.ds(r0[t], ck), :] = dts[t]
                dtt_scr[cc[t]] = dts[t].T[0:DT_ROWS, :]
            return carry

        lax.fori_loop(0, nc // eper, conv, 0)

        def chunk(i, carry):
            dirs = (0, 1)
            streams = [(t, d) for t in range(per) for d in dirs]
            cidx = {(t, d): (i * per + t if d == 0 else nc - 1 - (i * per + t)) for (t, d) in streams}
            r0 = {s: pl.multiple_of(cidx[s] * ck, ck) for s in streams}
            cb16 = {s: cm_scr[pl.ds(r0[s], ck), :] for s in streams}
            bt32 = {s: bt_scr[cidx[s]] for s in streams}
            dt = {s: dts_scr[pl.ds(r0[s], ck), :] for s in streams}
            dt_t = {s: dtt_scr[cidx[s]] for s in streams}
            sp = {s: _split3(dt[s] * a_ref[0]) for s in streams}
            sp_t = {s: _split3(dt_t[s] * acol_ref[0, 0]) for s in streams}
            c3 = {s: jnp.dot(tri_bf[s[1]], jnp.concatenate(sp[s], axis=1), preferred_element_type=f32)
                  for s in streams}
            cum = {s: c3[s][:, 0:LANES] + c3[s][:, LANES:2 * LANES] + c3[s][:, 2 * LANES:] for s in streams}
            c3t = {s: jnp.dot(jnp.concatenate(sp_t[s], axis=0), tri_bf[1 - s[1]], preferred_element_type=f32)
                   for s in streams}
            cum_t = {s: c3t[s][0:DT_ROWS] + c3t[s][DT_ROWS:2 * DT_ROWS] + c3t[s][2 * DT_ROWS:] for s in streams}
            cbm = {s: jnp.dot(cb16[s], bt32[s].astype(bf16), preferred_element_type=f32) for s in streams}
            tot = {s: (cum[s][ck - 1:ck, :] if s[1] == 0 else cum[s][0:1, :]) for s in streams}
            tot_t = {s: (cum_t[s][:, ck - 1:ck] if s[1] == 0 else cum_t[s][:, 0:1]) for s in streams}
            w_t = {s: dt_t[s] * jnp.exp(tot_t[s] - cum_t[s]) for s in streams}
            etot = {s: jnp.exp(tot[s]) for s in streams}
            pairs = range(SSD_KH // 2)
            units = [(t, p, d) for t in range(per) for p in pairs for d in dirs]
            xp16 = {(t, p, d): xs_scr[pl.ds(r0[t, d], ck), p * LANES:(p + 1) * LANES] for (t, p, d) in units}
            mks, ccols, bws = {}, {}, {}
            for kk in range(2):
                for (t, p, d) in units:
                    s = (t, d)
                    j = d * SSD_KH + 2 * p + kk
                    ccols[t, p, d, kk] = jnp.broadcast_to(cum[s][:, j:j + 1], (ck, ck))
                    decay = jnp.exp(jnp.where(tri[d], ccols[t, p, d, kk] - cum_t[s][j:j + 1, :], NEG_INF))
                    mks[t, p, d, kk] = (cbm[s] * decay * dt_t[s][j:j + 1, :]).astype(bf16)
                    bws[t, p, d, kk] = (bt32[s] * w_t[s][j:j + 1, :]).astype(bf16)
            yy, ss = {}, {}
            for u in units:
                yy[u] = jnp.dot(jnp.concatenate([mks[u + (0,)], mks[u + (1,)]], axis=0), xp16[u],
                                preferred_element_type=f32)
                ss[u] = jnp.dot(jnp.concatenate([bws[u + (0,)], bws[u + (1,)]], axis=0), xp16[u],
                                preferred_element_type=f32)
            hcur = {(p, d): h_scr[d * (SSD_KH // 2) + p] for p in pairs for d in dirs}
            for (t, p, d) in units:
                s, u = (t, d), (t, p, d)
                j0 = d * SSD_KH + 2 * p
                ch = jnp.dot(cb16[s], hcur[p, d].astype(bf16), preferred_element_type=f32)
                ydiag = jnp.where(first, yy[u][0:ck], yy[u][ck:])
                yoff = jnp.exp(jnp.where(first, ccols[u + (0,)], ccols[u + (1,)])) * ch
                ydir_scr[d, pl.ds(r0[s], ck), p * LANES:(p + 1) * LANES] = ydiag + yoff
                erow = jnp.where(first[0:1, :], etot[s][:, j0:j0 + 1], etot[s][:, j0 + 1:j0 + 2])
                hcur[p, d] = hcur[p, d] * erow + jnp.where(first, ss[u][0:nst], ss[u][nst:])
            for p in pairs:
                for d in dirs:
                    h_scr[d * (SSD_KH // 2) + p] = hcur[p, d]
            return carry

        lax.fori_loop(0, nc // per, chunk, 0)

        def gate(c, carry):
            sub = range(eper)
            r0 = [pl.multiple_of((c * eper + t) * ck, ck) for t in sub]
            y = [y_scr[pl.ds(r0[t], ck), :] + ydir_scr[0, pl.ds(r0[t], ck), :] + ydir_scr[1, pl.ds(r0[t], ck), :]
                 for t in sub]
            u = [y[t] * _silu(z_ref[0, pl.ds(r0[t], ck), :].astype(f32)) for t in sub]
            ms = [jnp.mean(u[t] * u[t], axis=-1, keepdims=True) for t in sub]
            u = [u[t] * lax.rsqrt(ms[t] + NORM_EPS) for t in sub]
            for t in sub:
                o_ref[0, pl.ds(r0[t], ck), :] = (u[t] * ng_ref[...]).astype(bf16)
            return carry

        lax.fori_loop(0, nc // eper, gate, 0)

    run_seq(lc, xc_ref, bc_ref, cc_ref, dtc_ref, zc_ref, oc_ref)
    run_seq(seq, xl_ref, bl_ref, cl_ref, dtl_ref, zl_ref, ol_ref)


def _ssd(xbc_c, dt_c, z_c, xbc_l, dt_l, z_l, cw, cb, dtb, a_neg, a_col, dsk, ng):
    b, lc, _ = xbc_c.shape
    seq = xbc_l.shape[1]
    gw, nst = SSD_GW, SSD_STATE
    xoff = SSD_W // nst

    def seq_specs(t):
        return [pl.BlockSpec((1, t, gw), lambda bi, g: (bi, 0, g)),
                pl.BlockSpec((1, t, nst), lambda bi, g: (bi, 0, xoff + g)),
                pl.BlockSpec((1, t, nst), lambda bi, g: (bi, 0, xoff + SSD_GROUPS + g)),
                pl.BlockSpec((1, t, LANES), lambda bi, g: (bi, 0, g)),
                pl.BlockSpec((1, t, gw), lambda bi, g: (bi, 0, g))]

    cwid = gw + 2 * nst
    par_specs = [pl.BlockSpec((1, 3, cwid), lambda bi, g: (g, 0, 0)),
                 pl.BlockSpec((1, 1, cwid), lambda bi, g: (g, 0, 0)),
                 pl.BlockSpec((1, 1, LANES), lambda bi, g: (g, 0, 0)),
                 pl.BlockSpec((1, 1, LANES), lambda bi, g: (g, 0, 0)),
                 pl.BlockSpec((1, 1, DT_ROWS, 1), lambda bi, g: (g, 0, 0, 0)),
                 pl.BlockSpec((1, gw), lambda bi, g: (0, g)),
                 pl.BlockSpec((1, gw), lambda bi, g: (0, g))]
    nck = seq // SSD_CHUNK
    return pl.pallas_call(
        functools.partial(_ssd_kernel, lc=lc, seq=seq),
        grid=(b, SSD_GROUPS),
        in_specs=seq_specs(lc) + seq_specs(seq) + par_specs,
        out_specs=[pl.BlockSpec((1, lc, gw), lambda bi, g: (bi, 0, g)),
                   pl.BlockSpec((1, seq, gw), lambda bi, g: (bi, 0, g))],
        out_shape=[jax.ShapeDtypeStruct((b, lc, SSD_W), bf16), jax.ShapeDtypeStruct((b, seq, SSD_W), bf16)],
        scratch_shapes=[pltpu.VMEM((seq + 16, cwid), f32),
                        pltpu.VMEM((seq, gw), bf16),
                        pltpu.VMEM((seq, nst), bf16),
                        pltpu.VMEM((nck, nst, SSD_CHUNK), f32),
                        pltpu.VMEM((seq, LANES), f32),
                        pltpu.VMEM((nck, DT_ROWS, SSD_CHUNK), f32),
                        pltpu.VMEM((seq, gw), f32),
                        pltpu.VMEM((2, seq, gw), f32),
                        pltpu.VMEM((SSD_KH, nst, LANES), f32)],
        compiler_params=_cparams(("arbitrary", "arbitrary")),
        name="ssd",
    )(xbc_c, xbc_c, xbc_c, dt_c, z_c, xbc_l, xbc_l, xbc_l, dt_l, z_l, cw, cb, dtb, a_neg, a_col, dsk, ng)


def _sconv_kernel(b_ref, c_ref, h_ref, w_ref, o_ref, pad_scr, *, t):
    ck = min(256, t)
    zero8 = jnp.zeros((8, LANES), f32)
    pad_scr[0:8, :] = zero8
    pad_scr[8 + t:16 + t, :] = zero8

    def fill(c, carry):
        r0 = pl.multiple_of(c * ck, ck)
        pad_scr[pl.ds(r0 + 8, ck), :] = c_ref[0, pl.ds(r0, ck), :].astype(f32) * h_ref[0, pl.ds(r0, ck), :].astype(f32)
        return carry

    lax.fori_loop(0, t // ck, fill, 0)

    def conv(c, carry):
        r0 = pl.multiple_of(c * ck, ck)
        xm, x0, xp = _shifted_rows(pad_scr, r0, ck)
        acc = w_ref[0:1, :] * xm + w_ref[1:2, :] * x0 + w_ref[2:3, :] * xp
        o_ref[0, pl.ds(r0, ck), :] = (b_ref[0, pl.ds(r0, ck), :].astype(f32) * acc).astype(bf16)
        return carry

    lax.fori_loop(0, t // ck, conv, 0)


def _sconv(scb, scc, sch, w):
    b, t, cw = scb.shape
    spec = pl.BlockSpec((1, t, LANES), lambda bi, j: (bi, 0, j))
    return pl.pallas_call(
        functools.partial(_sconv_kernel, t=t),
        grid=(b, cw // LANES),
        in_specs=[spec, spec, spec, pl.BlockSpec((3, LANES), lambda bi, j: (0, j))],
        out_specs=spec,
        out_shape=jax.ShapeDtypeStruct((b, t, cw), bf16),
        scratch_shapes=[pltpu.VMEM((t + 16, LANES), f32)],
        compiler_params=_cparams(("arbitrary", "arbitrary")),
        name="sconv",
    )(scb, scc, sch, w)


def _outproj_kernel(att_ref, ssd_ref, sc_ref, x_ref, g1_ref, sh2_ref, sc2_ref, n2_ref, wo_ref, wr_ref,
                    xo_ref, h2_ref, lg_ref, *, sub):
    for r0 in range(0, x_ref.shape[0], sub):
        rs = slice(r0, r0 + sub)
        acc = jnp.dot(att_ref[rs, :], wo_ref[0, 0:ATT_W, :], preferred_element_type=f32)
        acc = acc + jnp.dot(ssd_ref[rs, :], wo_ref[0, ATT_W:ATT_W + SSD_W, :], preferred_element_type=f32)
        acc = acc + jnp.dot(sc_ref[rs, :], wo_ref[0, ATT_W + SSD_W:, :], preferred_element_type=f32)
        x = x_ref[rs, :] + g1_ref[0] * acc
        xo_ref[rs, :] = x
        y = x * lax.rsqrt(jnp.mean(x * x, axis=-1, keepdims=True) + NORM_EPS) * n2_ref[...]
        h2 = y * (1.0 + sc2_ref[0]) + sh2_ref[0]
        hh = h2.astype(bf16)
        h2_ref[rs, :] = hh
        hm = (h2 - hh.astype(f32)).astype(bf16)
        r1 = jnp.dot(hh, wr_ref[0], preferred_element_type=f32)
        r2 = jnp.dot(hm, wr_ref[0, :, 0:ROUTER_PAD], preferred_element_type=f32)
        lg_ref[rs, :] = r1[:, 0:ROUTER_PAD] + r1[:, ROUTER_PAD:] + r2


def _outproj(att, ssd, sconv, x2d, g1, sh2, sc2, n2, wo, wr2, layer, tm, rows_per_mod):
    m, d = x2d.shape
    tpm = rows_per_mod // tm
    mod_spec = pl.BlockSpec((1, 1, d), lambda i: (i // tpm, 0, 0))
    return pl.pallas_call(
        functools.partial(_outproj_kernel, sub=min(256, tm)),
        grid=(m // tm,),
        in_specs=[pl.BlockSpec((tm, ATT_W), lambda i: (i, 0)), pl.BlockSpec((tm, SSD_W), lambda i: (i, 0)),
                  pl.BlockSpec((tm, SC_W), lambda i: (i, 0)), pl.BlockSpec((tm, d), lambda i: (i, 0)),
                  mod_spec, mod_spec, mod_spec, pl.BlockSpec((1, d), lambda i: (0, 0)),
                  pl.BlockSpec((1, d, d), lambda i: (layer, 0, 0)),
                  pl.BlockSpec((1, d, 2 * ROUTER_PAD), lambda i: (layer, 0, 0))],
        out_specs=[pl.BlockSpec((tm, d), lambda i: (i, 0)), pl.BlockSpec((tm, d), lambda i: (i, 0)),
                   pl.BlockSpec((tm, ROUTER_PAD), lambda i: (i, 0))],
        out_shape=[jax.ShapeDtypeStruct((m, d), f32), jax.ShapeDtypeStruct((m, d), bf16),
                   jax.ShapeDtypeStruct((m, ROUTER_PAD), f32)],
        compiler_params=_cparams(("arbitrary",)),
        name="outproj",
    )(att, ssd, sconv, x2d, g1, sh2, sc2, n2, wo, wr2)


def _ffn_kernel(*refs, nt, with_ctx):
    if with_ctx:
        x_ref, gv_ref, xc_ref, gvc_ref, wg_ref, wu_ref, wd_ref, o_ref, oc_ref = refs
    else:
        x_ref, gv_ref, wg_ref, wu_ref, wd_ref, o_ref = refs

    def ffn(xr, gr, outr):
        x = xr[0]
        a = jnp.dot(x, wg_ref[0, 0].astype(bf16), preferred_element_type=f32)
        u = jnp.dot(x, wu_ref[0, 0].astype(bf16), preferred_element_type=f32)
        hmid = (_silu(a) * u).astype(bf16)
        outr[0] = (jnp.dot(hmid, wd_ref[0, 0].astype(bf16), preferred_element_type=f32) * gr[0]).astype(bf16)

    if not with_ctx:
        ffn(x_ref, gv_ref, o_ref)
        return
    i = pl.program_id(1)

    @pl.when(i < nt)
    def _():
        ffn(x_ref, gv_ref, o_ref)

    @pl.when(i == nt)
    def _():
        ffn(xc_ref, gvc_ref, oc_ref)


def _ffn_tile(nrows):
    for nt in range(1, nrows + 1):
        if nrows % nt == 0 and (nrows // nt) % 16 == 0 and nrows // nt <= FFN_MAX_TILE:
            return nrows // nt
    raise ValueError(f"no aligned row tile for {nrows} gathered rows")


def _expert_ffn(xg, gv, wg, wu, wd, layer, xgc=None, gvc=None):
    e, r, d = xg.shape
    ff = wg.shape[3]
    tm = _ffn_tile(r)
    nt = r // tm
    with_ctx = xgc is not None
    last = nt - 1
    row_specs = [pl.BlockSpec((1, tm, d), lambda ei, i: (ei, jnp.minimum(i, last), 0)),
                 pl.BlockSpec((1, tm, 1), lambda ei, i: (ei, jnp.minimum(i, last), 0))]
    one = pl.Buffered(1)
    w_specs = [pl.BlockSpec((1, 1, d, ff), lambda ei, i: (layer, ei, 0, 0), pipeline_mode=one),
               pl.BlockSpec((1, 1, d, ff), lambda ei, i: (layer, ei, 0, 0)),
               pl.BlockSpec((1, 1, ff, d), lambda ei, i: (layer, ei, 0, 0))]
    out_specs = [pl.BlockSpec((1, tm, d), lambda ei, i: (ei, jnp.minimum(i, last), 0))]
    out_shape = [jax.ShapeDtypeStruct((e, r, d), bf16)]
    args = [xg, gv]
    if with_ctx:
        rc = xgc.shape[1]
        row_specs += [pl.BlockSpec((1, rc, d), lambda ei, i: (ei, 0, 0)),
                      pl.BlockSpec((1, rc, 1), lambda ei, i: (ei, 0, 0))]
        out_specs.append(pl.BlockSpec((1, rc, d), lambda ei, i: (ei, 0, 0)))
        out_shape.append(jax.ShapeDtypeStruct((e, rc, d), bf16))
        args += [xgc, gvc]
    outs = pl.pallas_call(
        functools.partial(_ffn_kernel, nt=nt, with_ctx=with_ctx),
        grid=(e, nt + (1 if with_ctx else 0)),
        in_specs=row_specs + w_specs,
        out_specs=out_specs,
        out_shape=out_shape,
        compiler_params=_cparams(("arbitrary", "arbitrary")),
        name="expert_ffn",
    )(*args, wg, wu, wd)
    return outs if with_ctx else outs[0]


def _combine_kernel(hit_ref, idx_ref, y_ref, xm_ref, g2_ref, o_ref, acc_scr, *, sb):
    bi = pl.program_id(0)
    k = pl.program_id(1)
    e = pl.program_id(2)
    nb, nk, ne = pl.num_programs(0), pl.num_programs(1), pl.num_programs(2)
    tt = acc_scr.shape[0]
    nsb = y_ref.shape[1] // sb

    @pl.when(e == 0)
    def _():
        acc_scr[...] = jnp.zeros(acc_scr.shape, f32)

    tok = lax.broadcasted_iota(jnp.int32, (tt, sb), 0) + k * tt
    for j in range(nsb):
        @pl.when(hit_ref[((e * nb + bi) * nk + k) * nsb + j] != 0)
        def _():
            onehot = jnp.where(tok == idx_ref[0, 0, :, j * sb:(j + 1) * sb], 1.0, 0.0).astype(bf16)
            acc_scr[...] += jnp.dot(onehot, y_ref[0, j * sb:(j + 1) * sb, :], preferred_element_type=f32)

    @pl.when(e == ne - 1)
    def _():
        o_ref[...] = xm_ref[...] + g2_ref[0] * acc_scr[...]


def _combine(idx, y, xm, g2, b, t, per_sample_gate):
    e, _, _, cap = idx.shape
    d = xm.shape[1]
    tt = min(COMBINE_TILE, t)
    nk = t // tt
    sb = min(COMBINE_SLOTS, cap)
    nsb = cap // sb
    blk = idx.reshape(e, b, nsb, sb)
    lo = blk[..., 0][:, :, None, :]
    hi = blk[..., sb - 1][:, :, None, :]
    tile0 = (jnp.arange(nk, dtype=idx.dtype) * tt)[None, None, :, None]
    hit = ((lo < tile0 + tt) & (hi >= tile0)).astype(jnp.int32).reshape(-1)
    grid_spec = pltpu.PrefetchScalarGridSpec(
        num_scalar_prefetch=1,
        grid=(b, nk, e),
        in_specs=[pl.BlockSpec((1, 1, 1, cap), lambda bi, k, ei, hit_ref: (ei, bi, 0, 0)),
                  pl.BlockSpec((1, cap, d), lambda bi, k, ei, hit_ref: (ei, bi, 0)),
                  pl.BlockSpec((tt, d), lambda bi, k, ei, hit_ref: (bi * nk + k, 0)),
                  pl.BlockSpec((1, 1, d), lambda bi, k, ei, hit_ref: (bi if per_sample_gate else 0, 0, 0))],
        out_specs=pl.BlockSpec((tt, d), lambda bi, k, ei, hit_ref: (bi * nk + k, 0)),
        scratch_shapes=[pltpu.VMEM((tt, d), f32)])
    return pl.pallas_call(
        functools.partial(_combine_kernel, sb=sb),
        grid_spec=grid_spec,
        out_shape=jax.ShapeDtypeStruct(xm.shape, f32),
        compiler_params=_cparams(("arbitrary", "arbitrary", "arbitrary")),
        name="moe_combine",
    )(hit, idx, y, xm, g2)


def _rope_tables(n_tokens):
    rows = n_tokens // GRID_W
    row = jnp.repeat(jnp.arange(rows), GRID_W).astype(f32)
    col = jnp.tile(jnp.arange(GRID_W), rows).astype(f32)
    n_freq = HEAD_DIM // 4
    inv = ROPE_THETA ** (-jnp.arange(n_freq, dtype=f32) / n_freq)
    ang_r = row[:, None] * inv
    ang_c = col[:, None] * inv
    cos = jnp.concatenate([jnp.cos(ang_r), jnp.cos(ang_r), jnp.cos(ang_c), jnp.cos(ang_c)], axis=-1)
    sins = jnp.concatenate([-jnp.sin(ang_r), jnp.sin(ang_r), -jnp.sin(ang_c), jnp.sin(ang_c)], axis=-1)
    return cos, sins


def _dt_layout(v):
    lead = v.shape[:-1]
    v = v.reshape(lead + (2, SSD_GROUPS, SSD_KH))
    v = jnp.moveaxis(v, -2, -3).reshape(lead + (SSD_GROUPS, 2 * SSD_KH))
    v = jnp.pad(v, [(0, 0)] * (len(lead) + 1) + [(0, LANES - 2 * SSD_KH)])
    return v.reshape(lead + (DT_PAD,))


def _group_layout(v):
    outs = []
    for g in range(SSD_GROUPS):
        outs.append(jnp.concatenate([
            v[..., g * SSD_GW:(g + 1) * SSD_GW],
            v[..., SSD_W + g * SSD_STATE:SSD_W + (g + 1) * SSD_STATE],
            v[..., SSD_W + SSD_BC_W + g * SSD_STATE:SSD_W + SSD_BC_W + (g + 1) * SSD_STATE]], axis=-1))
    return jnp.stack(outs)


def _route(logits, b, t):
    cap = EC_FACTOR * t // N_EXPERTS
    aff = jax.nn.softmax(logits[:, :N_EXPERTS].reshape(b, t, N_EXPERTS), axis=-1)
    gval, idx = lax.top_k(jnp.swapaxes(aff, 1, 2), cap)
    idx, gval = lax.sort((idx, gval), dimension=-1, num_keys=1)
    rows = idx + (jnp.arange(b, dtype=idx.dtype) * t)[:, None, None]
    to_e = lambda u: jnp.swapaxes(u, 0, 1).reshape(N_EXPERTS, b * cap)
    return to_e(gval)[..., None], to_e(rows), jnp.swapaxes(idx, 0, 1).reshape(N_EXPERTS, b, 1, cap)


def kernel(x, c, ctx, c_ctx, norm1_g, norm2_g, w_mod, b_mod, w_in, q_norm_g, k_norm_g, attn_sink, ssd_conv_w,
           ssd_conv_b, ssd_dt_bias, ssd_a_log, ssd_d, ssd_norm_g, sc_conv_w, w_out, w_router, w_expert_gate,
           w_expert_up, w_expert_down):
    b, s, d = x.shape
    lc = ctx.shape[1]
    nl = w_in.shape[0]
    cos, sins = _rope_tables(s)

    nrow = -(-(b + 1) // 8) * 8
    cvec = jnp.zeros((nrow, d), f32).at[:b].set(c).at[b].set(c_ctx)
    mods = _modulations(cvec, w_mod, b_mod).reshape(nl, nrow, N_MOD, 1, d)

    cuts = [0]
    for wdt in (ATT_W, ATT_KV_W, ATT_KV_W, SSD_W, SSD_XBC_W, SSD_DT_W, SC_W, SC_W, SC_W):
        cuts.append(cuts[-1] + wdt)
    w_dt = _dt_layout(w_in[:, :, cuts[5]:cuts[6]])
    w_in_p = jnp.concatenate([w_in[:, :, :cuts[5]], w_dt, w_in[:, :, cuts[6]:]], axis=-1).astype(bf16)
    w_out_b = w_out.astype(bf16)
    wr_pad = jnp.pad(w_router, ((0, 0), (0, 0), (0, ROUTER_PAD - N_EXPERTS)))
    wr_hi, wr_mid, _ = _split3(wr_pad)
    wr2 = jnp.concatenate([wr_hi, wr_mid], axis=-1)
    dtb = _dt_layout(ssd_dt_bias.reshape(nl, SSD_DT_W)).reshape(nl, SSD_GROUPS, 1, LANES)
    a_neg = _dt_layout(-jnp.exp(ssd_a_log.reshape(nl, SSD_DT_W))).reshape(nl, SSD_GROUPS, 1, LANES)
    a_col = a_neg[..., :DT_ROWS].reshape(nl, SSD_GROUPS, 1, DT_ROWS, 1)
    dsk = jnp.repeat(ssd_d, SSD_HEAD_DIM, axis=-1).reshape(nl, 1, SSD_W)

    xl = x.reshape(b * s, d)
    xc = ctx.reshape(b * lc, d)
    for i in range(nl):
        with_ctx_out = i < nl - 1
        sh1, sc1, g1, sh2, sc2, g2 = (mods[i, :, j] for j in range(N_MOD))
        n1 = norm1_g[i].reshape(1, d)
        n2 = norm2_g[i].reshape(1, d)
        qg = q_norm_g[i].reshape(1, HEAD_DIM)
        kg = k_norm_g[i].reshape(1, HEAD_DIM)

        q, k, v, z, xbc, dtr, scb, scc, sch = _inproj(xl, sh1, sc1, n1, w_in_p, i, 512, s)
        qc, kc, vc, zc, xbcc, dtrc, scbc, sccc, schc = _inproj(xc, sh1[b:], sc1[b:], n1, w_in_p, i, lc, b * lc)

        kp = _kprep(k, kg, cos, sins, s, True)
        kcp = _kprep(kc, kg, cos, sins, lc, False)
        r3 = lambda u, t: u.reshape(b, t, u.shape[-1])
        att = _attention(r3(q, s), qg, attn_sink[i], r3(kcp, lc), r3(vc, lc), r3(kp, s), r3(v, s), cos, sins)

        ssd_c, ssd_l = _ssd(r3(xbcc, lc), r3(dtrc, lc), r3(zc, lc), r3(xbc, s), r3(dtr, s), r3(z, s),
                            _group_layout(ssd_conv_w[i]), _group_layout(ssd_conv_b[i].reshape(1, -1)),
                            dtb[i], a_neg[i], a_col[i], dsk[i], ssd_norm_g[i].reshape(1, SSD_W))
        sconv = _sconv(r3(scb, s), r3(scc, s), r3(sch, s), sc_conv_w[i])

        xm, h2, lg = _outproj(att.reshape(b * s, ATT_W), ssd_l.reshape(b * s, SSD_W), sconv.reshape(b * s, SC_W),
                              xl, g1, sh2, sc2, n2, w_out_b, wr2, i, 512, s)
        gv, rows, idx = _route(lg, b, s)
        if with_ctx_out:
            attc = _attention(r3(qc, lc), qg, attn_sink[i], r3(kcp, lc), r3(vc, lc))
            sconvc = _sconv(r3(scbc, lc), r3(sccc, lc), r3(schc, lc), sc_conv_w[i])
            xmc, h2c, lgc = _outproj(attc.reshape(b * lc, ATT_W), ssd_c.reshape(b * lc, SSD_W),
                                     sconvc.reshape(b * lc, SC_W), xc, g1[b:], sh2[b:], sc2[b:], n2,
                                     w_out_b, wr2, i, lc, b * lc)
            gvc, rowsc, idxc = _route(lgc, b, lc)
            y, yc = _expert_ffn(h2[rows], gv, w_expert_gate, w_expert_up, w_expert_down, i, h2c[rowsc], gvc)
            xc = _combine(idxc, yc, xmc, g2[b:], b, lc, False)
        else:
            y = _expert_ffn(h2[rows], gv, w_expert_gate, w_expert_up, w_expert_down, i)
        xl = _combine(idx, y, xm, g2, b, s, True)
    return xl.reshape(b, s, d)
```
